```python
import jax
import jax.numpy as jnp
from jax import lax
import numpy as np

D_MODEL = 1024
BATCH = 8
SEQ = 4096
DEPTH = 4

GRID_W = 64
CTX_LEN = 256
N_MIXERS = 2
N_ATTN_LAYERS = (DEPTH + N_MIXERS - 1) // N_MIXERS
N_GLA_LAYERS = DEPTH // N_MIXERS
N_MOD = 6
RMS_EPS = 1e-6

ATTN_HEAD_DIM = 64
N_Q_HEADS = D_MODEL // ATTN_HEAD_DIM
N_KV_HEADS = N_Q_HEADS // 4
GQA_GROUP = N_Q_HEADS // N_KV_HEADS
QKV_DIM = (N_Q_HEADS + 2 * N_KV_HEADS) * ATTN_HEAD_DIM
ROPE_THETA = 10000.0
Q_BLOCK = 128

GLA_HEADS = 4
GLA_DK = D_MODEL // 2
GLA_DV = D_MODEL
GLA_DK_HEAD = GLA_DK // GLA_HEADS
GLA_DV_HEAD = GLA_DV // GLA_HEADS
GLA_GATE_RANK = 16
GLA_GATE_NORM = 16.0
GLA_CHUNK = 64
GLA_IN_DIM = 2 * GLA_DK + 2 * GLA_DV + 2 * GLA_GATE_RANK

N_EXPERTS = 32
TOP_K = 4
D_EXPERT = D_MODEL
SWIGLU_LIMIT = 7.0
SWIGLU_ALPHA = 1.702

kernel_name = 'hybrid_gqa_gla_moe_dit'


def rms_norm(x, g):
    xf = x.astype(jnp.float32)
    y = xf * lax.rsqrt(jnp.mean(xf * xf, axis=-1, keepdims=True) + RMS_EPS)
    return (y * g.astype(jnp.float32)).astype(x.dtype)


def ada_modulation(cond, w_mod, b_mod):
    m = jax.nn.silu(cond) @ w_mod + b_mod
    return jnp.split(m, N_MOD, axis=-1)


def modulate(x, g, shift, scale):
    return rms_norm(x, g) * (1.0 + scale) + shift


def axial_rope_tables(rows):
    row_ids = jnp.repeat(jnp.arange(rows), GRID_W).astype(jnp.float32)
    col_ids = jnp.tile(jnp.arange(GRID_W), rows).astype(jnp.float32)
    axis_dim = ATTN_HEAD_DIM // 2
    inv_freq = 1.0 / (ROPE_THETA ** (jnp.arange(0, axis_dim, 2, dtype=jnp.float32) / axis_dim))
    ang = jnp.concatenate([row_ids[:, None] * inv_freq, col_ids[:, None] * inv_freq], axis=-1)
    return jnp.cos(ang), jnp.sin(ang)


def apply_rope(x, cos, sin):
    extra = x.ndim - 3
    c = cos.reshape((cos.shape[0],) + (1,) * extra + (cos.shape[1],))
    s = sin.reshape((sin.shape[0],) + (1,) * extra + (sin.shape[1],))
    x1 = x[..., 0::2]
    x2 = x[..., 1::2]
    y = jnp.stack([x1 * c - x2 * s, x1 * s + x2 * c], axis=-1)
    return y.reshape(x.shape).astype(x.dtype)


def gqa_attend(q, k, v):
    s = jnp.einsum('bqhgd,bkhd->bhgqk', q, k).astype(jnp.float32) * (ATTN_HEAD_DIM ** -0.5)
    p = jax.nn.softmax(s, axis=-1).astype(v.dtype)
    return jnp.einsum('bhgqk,bkhd->bqhgd', p, v)


def qkv_heads(h, w_qkv, q_gain, k_gain):
    B, T, _ = h.shape
    q, k, v = jnp.split(h @ w_qkv, [N_Q_HEADS * ATTN_HEAD_DIM, (N_Q_HEADS + N_KV_HEADS) * ATTN_HEAD_DIM], axis=-1)
    q = rms_norm(q.reshape(B, T, N_KV_HEADS, GQA_GROUP, ATTN_HEAD_DIM), q_gain)
    k = rms_norm(k.reshape(B, T, N_KV_HEADS, ATTN_HEAD_DIM), k_gain)
    v = v.reshape(B, T, N_KV_HEADS, ATTN_HEAD_DIM)
    return q, k, v


def attention_mixer(h_ctx, h_lat, cos, sin, w_qkv, q_gain, k_gain, w_o, ctx_out):
    B, S, _ = h_lat.shape
    L = h_ctx.shape[1]
    q_c, k_c, v_c = qkv_heads(h_ctx, w_qkv, q_gain, k_gain)
    q_l, k_l, v_l = qkv_heads(h_lat, w_qkv, q_gain, k_gain)
    q_l = apply_rope(q_l, cos, sin)
    k_l = apply_rope(k_l, cos, sin)
    k_all = jnp.concatenate([k_c, k_l], axis=1)
    v_all = jnp.concatenate([v_c, v_l], axis=1)
    n_blk = S // Q_BLOCK
    q_blocks = q_l.reshape(B, n_blk, Q_BLOCK, N_KV_HEADS, GQA_GROUP, ATTN_HEAD_DIM).swapaxes(0, 1)
    o_blocks = lax.map(lambda qb: gqa_attend(qb, k_all, v_all), q_blocks)
    o_lat = o_blocks.swapaxes(0, 1).reshape(B, S, D_MODEL) @ w_o
    o_ctx = None
    if ctx_out:
        o_ctx = gqa_attend(q_c, k_c, v_c).reshape(B, L, D_MODEL) @ w_o
    return o_ctx, o_lat


def gla_chunk_scan(q, k, v, g, s0):
    B, H, T, dk = q.shape
    dv = v.shape[-1]
    n = T // GLA_CHUNK

    def to_chunks(a):
        return a.reshape(B, H, n, GLA_CHUNK, a.shape[-1]).transpose(2, 0, 1, 3, 4)

    mask = jnp.tril(jnp.ones((GLA_CHUNK, GLA_CHUNK), dtype=bool))[:, :, None]

    def step(state, inp):
        qc, kc, vc, gc = inp
        b = jnp.cumsum(gc, axis=2)
        o_inter = jnp.einsum('bhcd,bhde->bhce', qc * jnp.exp(b), state)
        diff = b[:, :, :, None, :] - b[:, :, None, :, :]
        decay = jnp.exp(jnp.where(mask, diff, -jnp.inf))
        A = jnp.einsum('bhid,bhjd,bhijd->bhij', qc, kc, decay)
        o_intra = jnp.einsum('bhij,bhje->bhie', A, vc)
        b_last = b[:, :, -1:, :]
        k_dec = kc * jnp.exp(b_last - b)
        new_state = jnp.exp(b_last[:, :, 0, :])[..., None] * state + jnp.einsum('bhcd,bhce->bhde', k_dec, vc)
        return new_state, o_inter + o_intra

    s_fin, o = lax.scan(step, s0, (to_chunks(q), to_chunks(k), to_chunks(v), to_chunks(g)))
    o = o.transpose(1, 2, 0, 3, 4).reshape(B, H, T, dv)
    return o, s_fin


def gla_mixer(h_ctx, h_lat, w_in, w_gk2_f, b_gk_f, w_gk2_b, b_gk_b, o_gain, w_o, ctx_out):
    splits = [GLA_DK, 2 * GLA_DK, 2 * GLA_DK + GLA_DV, 2 * GLA_DK + 2 * GLA_DV,
              2 * GLA_DK + 2 * GLA_DV + GLA_GATE_RANK]

    def project(h):
        B, T, _ = h.shape
        q, k, v, r, lf, lb = jnp.split(h @ w_in, splits, axis=-1)
        gf = jax.nn.log_sigmoid((lf @ w_gk2_f + b_gk_f).astype(jnp.float32)) / GLA_GATE_NORM
        gb = jax.nn.log_sigmoid((lb @ w_gk2_b + b_gk_b).astype(jnp.float32)) / GLA_GATE_NORM

        def heads(a, dh):
            return a.reshape(B, T, GLA_HEADS, dh).transpose(0, 2, 1, 3).astype(jnp.float32)

        qkvg = (heads(q, GLA_DK_HEAD) * (GLA_DK_HEAD ** -0.5), heads(k, GLA_DK_HEAD),
                heads(v, GLA_DV_HEAD), heads(gf, GLA_DK_HEAD), heads(gb, GLA_DK_HEAD))
        return qkvg, r

    def bidir(qkvg, sf0, sb0):
        q, k, v, gf, gb = qkvg
        flip = lambda a: jnp.flip(a, axis=2)
        o_f, s_f = gla_chunk_scan(q, k, v, gf, sf0)
        o_b, s_b = gla_chunk_scan(flip(q), flip(k), flip(v), flip(gb), sb0)
        return o_f + flip(o_b), s_f, s_b

    def readout(o, r):
        B, H, T, _ = o.shape
        o = rms_norm(o.transpose(0, 2, 1, 3), o_gain).reshape(B, T, GLA_DV)
        return (o.astype(r.dtype) * jax.nn.silu(r)) @ w_o

    qkvg_c, r_c = project(h_ctx)
    qkvg_l, r_l = project(h_lat)
    B = h_lat.shape[0]
    zero = jnp.zeros((B, GLA_HEADS, GLA_DK_HEAD, GLA_DV_HEAD), jnp.float32)
    o_c, s_f, s_b = bidir(qkvg_c, zero, zero)
    o_l, _, _ = bidir(qkvg_l, s_f, s_b)
    o_ctx = readout(o_c, r_c) if ctx_out else None
    return o_ctx, readout(o_l, r_l)


def moe_ffn(h, w_router, b_router, w_gu, b_gu, w_down, b_down):
    shape = h.shape
    hf = h.reshape(-1, shape[-1])
    logits = (hf @ w_router + b_router).astype(jnp.float32)
    top_v, top_i = lax.top_k(logits, TOP_K)
    top_w = jax.nn.softmax(top_v, axis=-1)
    gates = jnp.einsum('nk,nke->ne', top_w, jax.nn.one_hot(top_i, N_EXPERTS, dtype=jnp.float32))
    out = jnp.zeros(hf.shape, jnp.float32)
    for e in range(N_EXPERTS):
        gu = hf @ w_gu[e] + b_gu[e]
        gate = jnp.minimum(gu[:, :D_EXPERT], SWIGLU_LIMIT)
        up = jnp.clip(gu[:, D_EXPERT:], -SWIGLU_LIMIT, SWIGLU_LIMIT)
        act = (up + 1.0) * (gate * jax.nn.sigmoid(SWIGLU_ALPHA * gate))
        out = out + gates[:, e:e + 1] * (act @ w_down[e] + b_down[e]).astype(jnp.float32)
    return out.astype(h.dtype).reshape(shape)


def setup_inputs(seed: int = 0) -> dict:
    key = jax.random.key(seed)
    ks = iter(jax.random.split(key, 32))

    def nrm(shape, scale):
        return jax.random.normal(next(ks), shape, jnp.float32) * scale

    def gain(shape):
        return 1.0 + nrm(shape, 0.02)

    D = D_MODEL
    return {
        'x': nrm((BATCH, SEQ, D), 1.0),
        'c': nrm((BATCH, D), 1.0),
        'ctx': nrm((BATCH, CTX_LEN, D), 1.0),
        'c_ctx': nrm((D,), 1.0),
        'ada_w': nrm((DEPTH, D, N_MOD * D), 0.5 * D ** -0.5),
        'ada_b': nrm((DEPTH, N_MOD * D), 0.02),
        'norm1_g': gain((DEPTH, D)),
        'norm2_g': gain((DEPTH, D)),
        'attn_w_qkv': nrm((N_ATTN_LAYERS, D, QKV_DIM), D ** -0.5),
        'attn_q_gain': gain((N_ATTN_LAYERS, ATTN_HEAD_DIM)),
        'attn_k_gain': gain((N_ATTN_LAYERS, ATTN_HEAD_DIM)),
        'attn_w_o': nrm((N_ATTN_LAYERS, D, D), D ** -0.5),
        'gla_w_in': nrm((N_GLA_LAYERS, D, GLA_IN_DIM), D ** -0.5),
        'gla_w_gk2_f': nrm((N_GLA_LAYERS, GLA_GATE_RANK, GLA_DK), GLA_GATE_RANK ** -0.5),
        'gla_b_gk_f': nrm((N_GLA_LAYERS, GLA_DK), 0.1),
        'gla_w_gk2_b': nrm((N_GLA_LAYERS, GLA_GATE_RANK, GLA_DK), GLA_GATE_RANK ** -0.5),
        'gla_b_gk_b': nrm((N_GLA_LAYERS, GLA_DK), 0.1),
        'gla_o_gain': gain((N_GLA_LAYERS, GLA_DV_HEAD)),
        'gla_w_o': nrm((N_GLA_LAYERS, D, D), D ** -0.5),
        'moe_w_router': nrm((DEPTH, D, N_EXPERTS), D ** -0.5),
        'moe_b_router': nrm((DEPTH, N_EXPERTS), 0.01),
        'moe_w_gu': nrm((DEPTH, N_EXPERTS, D, 2 * D_EXPERT), D ** -0.5),
        'moe_b_gu': nrm((DEPTH, N_EXPERTS, 2 * D_EXPERT), 0.02),
        'moe_w_down': nrm((DEPTH, N_EXPERTS, D_EXPERT, D), D_EXPERT ** -0.5),
        'moe_b_down': nrm((DEPTH, N_EXPERTS, D), 0.02),
    }


def reference(x, c, ctx, c_ctx, ada_w, ada_b, norm1_g, norm2_g,
              attn_w_qkv, attn_q_gain, attn_k_gain, attn_w_o,
              gla_w_in, gla_w_gk2_f, gla_b_gk_f, gla_w_gk2_b, gla_b_gk_b, gla_o_gain, gla_w_o,
              moe_w_router, moe_b_router, moe_w_gu, moe_b_gu, moe_w_down, moe_b_down):
    n_lat = x.shape[1]
    n_ctx = ctx.shape[1]
    rows = n_lat // GRID_W
    cos, sin = axial_rope_tables(rows)
    for i in range(DEPTH):
        last = i == DEPTH - 1
        sh1_l, sc1_l, g1_l, sh2_l, sc2_l, g2_l = ada_modulation(c[:, None, :], ada_w[i], ada_b[i])
        sh1_c, sc1_c, g1_c, sh2_c, sc2_c, g2_c = ada_modulation(c_ctx, ada_w[i], ada_b[i])
        h_lat = modulate(x, norm1_g[i], sh1_l, sc1_l)
        h_ctx = modulate(ctx, norm1_g[i], sh1_c, sc1_c)
        j = i // N_MIXERS
        if i % N_MIXERS == 0:
            o_ctx, o_lat = attention_mixer(h_ctx, h_lat, cos, sin, attn_w_qkv[j], attn_q_gain[j],
                                           attn_k_gain[j], attn_w_o[j], not last)
        else:
            o_ctx, o_lat = gla_mixer(h_ctx, h_lat, gla_w_in[j], gla_w_gk2_f[j], gla_b_gk_f[j],
                                     gla_w_gk2_b[j], gla_b_gk_b[j], gla_o_gain[j], gla_w_o[j], not last)
        x = x + g1_l * o_lat
        if last:
            f = moe_ffn(modulate(x, norm2_g[i], sh2_l, sc2_l), moe_w_router[i], moe_b_router[i],
                        moe_w_gu[i], moe_b_gu[i], moe_w_down[i], moe_b_down[i])
            x = x + g2_l * f
        else:
            ctx = ctx + g1_c * o_ctx
            h2 = jnp.concatenate([modulate(ctx, norm2_g[i], sh2_c, sc2_c),
                                  modulate(x, norm2_g[i], sh2_l, sc2_l)], axis=1)
            f = moe_ffn(h2, moe_w_router[i], moe_b_router[i], moe_w_gu[i], moe_b_gu[i],
                        moe_w_down[i], moe_b_down[i])
            ctx = ctx + g2_c * f[:, :n_ctx]
            x = x + g2_l * f[:, n_ctx:]
    return x
```

```python
import functools

import jax
import jax.numpy as jnp
from jax import lax
from jax.experimental import pallas as pl
from jax.experimental.pallas import tpu as pltpu

F32 = jnp.float32
BF16 = jnp.bfloat16

D_MODEL = 1024
DEPTH = 4
GRID_W = 64
N_MIXERS = 2
N_MOD = 6
RMS_EPS = 1e-6
HEAD_DIM = 64
N_Q_HEADS = D_MODEL // HEAD_DIM
N_KV_HEADS = N_Q_HEADS // 4
GQA_GROUP = N_Q_HEADS // N_KV_HEADS
Q_DIM = N_Q_HEADS * HEAD_DIM
KV_DIM = N_KV_HEADS * HEAD_DIM
QKV_DIM = Q_DIM + 2 * KV_DIM
ROPE_THETA = 10000.0
GLA_HEADS = 4
GLA_DK = D_MODEL // 2
GLA_DV = D_MODEL
GLA_DK_HEAD = GLA_DK // GLA_HEADS
GLA_DV_HEAD = GLA_DV // GLA_HEADS
GLA_GATE_RANK = 16
GLA_GATE_NORM = 16.0
N_EXPERTS = 32
TOP_K = 4
D_EXPERT = D_MODEL
SWIGLU_LIMIT = 7.0
SWIGLU_ALPHA = 1.702

_LANES = 128
_TT = 256
_GLA_CHUNK = 64
_TM = 256
_ADA_TN = 1536
_VMEM_LIMIT = 56 * 1024 * 1024
_NEG = -1e30


def _cparams(*sem):
    return pltpu.CompilerParams(dimension_semantics=sem, vmem_limit_bytes=_VMEM_LIMIT)


def _dot(a, b):
    return jnp.dot(a, b, preferred_element_type=F32)


def _dot_nt(a, b):
    return lax.dot_general(a, b, (((1,), (1,)), ((), ())), preferred_element_type=F32)


def _split_bf16(a):
    hi = a.astype(BF16)
    lo = (a - hi.astype(F32)).astype(BF16)
    return hi, lo


def _dot3(a, w):
    ah, al = _split_bf16(a)
    wh, wl = _split_bf16(w)
    return _dot(ah, wh) + _dot(ah, wl) + _dot(al, wh)


def _modulate(x, g, shift, scale):
    ms = jnp.mean(x * x, axis=-1, keepdims=True)
    y = x * lax.rsqrt(ms + RMS_EPS) * g
    return y * (1.0 + scale) + shift


def _sigmoid(x):
    return 1.0 / (1.0 + jnp.exp(-x))


def _ada_kernel(cond_ref, w_ref, b_ref, o_ref):
    c = cond_ref[...]
    o_ref[0] = _dot3(c * _sigmoid(c), w_ref[0]) + b_ref[0]


def _ada_table(cond, ada_w, ada_b):
    depth, d, n = ada_w.shape
    r = cond.shape[0]
    return pl.pallas_call(
        _ada_kernel,
        grid=(depth, n // _ADA_TN),
        in_specs=[
            pl.BlockSpec((r, d), lambda i, j: (0, 0)),
            pl.BlockSpec((1, d, _ADA_TN), lambda i, j: (i, 0, j)),
            pl.BlockSpec((1, 1, _ADA_TN), lambda i, j: (i, 0, j)),
        ],
        out_specs=pl.BlockSpec((1, r, _ADA_TN), lambda i, j: (i, 0, j)),
        out_shape=jax.ShapeDtypeStruct((depth, r, n), F32),
        compiler_params=_cparams("parallel", "parallel"),
        name="ada_table",
    )(cond, ada_w, ada_b.reshape(depth, 1, n))


def _qkv_kernel(x_ref, mod_ref, g_ref, wt_ref, gq_ref, gk_ref, cos_ref, sin_ref,
                qt_ref, k_ref, vt_ref):
    mod = mod_ref[0, 0]
    h = _modulate(x_ref[0], g_ref[...], mod[0:1], mod[1:2])
    ht = h.T.astype(BF16)
    qkvt = _dot(wt_ref[...], ht)
    c = cos_ref[...]
    s = sin_ref[...]
    half = HEAD_DIM // 2

    def norm_rope(blk, gain, scale):
        ms = jnp.mean(blk * blk, axis=0, keepdims=True)
        n = blk * lax.rsqrt(ms + RMS_EPS) * gain
        x1 = n[0:half]
        x2 = n[half:HEAD_DIM]
        return jnp.concatenate([x1 * c - x2 * s, x1 * s + x2 * c], axis=0) * scale

    gq = gq_ref[...]
    gk = gk_ref[...]
    for hh in range(N_Q_HEADS):
        r0 = hh * HEAD_DIM
        qt_ref[0, r0:r0 + HEAD_DIM, :] = norm_rope(
            qkvt[r0:r0 + HEAD_DIM], gq, HEAD_DIM ** -0.5).astype(BF16)
    kt = jnp.concatenate(
        [norm_rope(qkvt[Q_DIM + j * HEAD_DIM:Q_DIM + (j + 1) * HEAD_DIM], gk, 1.0)
         for j in range(N_KV_HEADS)], axis=0)
    k_ref[0] = kt.T.astype(BF16)
    vt_ref[0] = qkvt[Q_DIM + KV_DIM:QKV_DIM].astype(BF16)


def _qkv_project(xa, mod, g, wt, gq, gk, cos_t, sin_t, nct):
    b, t, d = xa.shape
    nt = t // _TT
    seg = lambda j: jnp.where(j >= nct, 1, 0)
    return pl.pallas_call(
        _qkv_kernel,
        grid=(b, nt),
        in_specs=[
            pl.BlockSpec((1, _TT, d), lambda i, j: (i, j, 0)),
            pl.BlockSpec((1, 1, N_MOD, d), lambda i, j: (i, seg(j), 0, 0)),
            pl.BlockSpec((1, d), lambda i, j: (0, 0)),
            pl.BlockSpec((QKV_DIM, d), lambda i, j: (0, 0)),
            pl.BlockSpec((HEAD_DIM, 1), lambda i, j: (0, 0)),
            pl.BlockSpec((HEAD_DIM, 1), lambda i, j: (0, 0)),
            pl.BlockSpec((HEAD_DIM // 2, _TT), lambda i, j: (0, j)),
            pl.BlockSpec((HEAD_DIM // 2, _TT), lambda i, j: (0, j)),
        ],
        out_specs=[
            pl.BlockSpec((1, Q_DIM, _TT), lambda i, j: (i, 0, j)),
            pl.BlockSpec((1, _TT, KV_DIM), lambda i, j: (i, j, 0)),
            pl.BlockSpec((1, KV_DIM, _TT), lambda i, j: (i, 0, j)),
        ],
        out_shape=[
            jax.ShapeDtypeStruct((b, Q_DIM, t), BF16),
            jax.ShapeDtypeStruct((b, t, KV_DIM), BF16),
            jax.ShapeDtypeStruct((b, KV_DIM, t), BF16),
        ],
        compiler_params=_cparams("parallel", "parallel"),
        name="attn_qkv",
    )(xa, mod, g, wt, gq, gk, cos_t, sin_t)


def _attn_kernel(qt_ref, k_ref, vt_ref, wo_ref, x_ref, mod_ref, o_ref, qe_ref, ot_ref,
                 *, nct, n_ctx):
    j = pl.program_id(1)
    t_all = k_ref.shape[1]

    def heads(nk):
        def one_head(h, carry):
            kvh = h // GQA_GROUP
            q0 = pl.multiple_of(h * HEAD_DIM, HEAD_DIM)
            k0 = pl.multiple_of(kvh * HEAD_DIM, HEAD_DIM)
            qe_ref[...] = jnp.zeros_like(qe_ref)
            qe_ref[pl.ds(k0, HEAD_DIM), :] = qt_ref[0, pl.ds(q0, HEAD_DIM), :]
            st = _dot(k_ref[0, 0:nk, :], qe_ref[...])
            m = jnp.max(st, axis=0, keepdims=True)
            p = jnp.exp(st - m)
            l = jnp.sum(p, axis=0, keepdims=True)
            o = _dot(vt_ref[0, pl.ds(k0, HEAD_DIM), 0:nk], p.astype(BF16))
            ot_ref[pl.ds(q0, HEAD_DIM), :] = o / l
            return carry

        lax.fori_loop(0, N_Q_HEADS, one_head, 0)

    @pl.when(j < nct)
    def _():
        heads(n_ctx)

    @pl.when(j >= nct)
    def _():
        heads(t_all)

    att = ot_ref[...].T.astype(BF16)
    out = _dot(att, wo_ref[...])
    o_ref[0] = x_ref[0] + mod_ref[0, 0][2:3] * out


def _attention(qt, k, vt, wo, xa, mod, nct):
    b, t, d = xa.shape
    nt = t // _TT
    seg = lambda j: jnp.where(j >= nct, 1, 0)
    kern = functools.partial(_attn_kernel, nct=nct, n_ctx=nct * _TT)
    return pl.pallas_call(
        kern,
        grid=(b, nt),
        in_specs=[
            pl.BlockSpec((1, Q_DIM, _TT), lambda i, j: (i, 0, j)),
            pl.BlockSpec((1, t, KV_DIM), lambda i, j: (i, 0, 0)),
            pl.BlockSpec((1, KV_DIM, t), lambda i, j: (i, 0, 0)),
            pl.BlockSpec((Q_DIM, d), lambda i, j: (0, 0)),
            pl.BlockSpec((1, _TT, d), lambda i, j: (i, j, 0)),
            pl.BlockSpec((1, 1, N_MOD, d), lambda i, j: (i, seg(j), 0, 0)),
        ],
        out_specs=pl.BlockSpec((1, _TT, d), lambda i, j: (i, j, 0)),
        out_shape=jax.ShapeDtypeStruct((b, t, d), F32),
        scratch_shapes=[pltpu.VMEM((KV_DIM, _TT), BF16), pltpu.VMEM((Q_DIM, _TT), F32)],
        compiler_params=_cparams("parallel", "arbitrary"),
        name="attn_core",
    )(qt, k, vt, wo, xa, mod)


def _gla_in_kernel(x_ref, mod_ref, g_ref, wm_ref, wl_ref, w2_ref, b2_ref,
                   q_ref, k_ref, v_ref, r_ref, gf_ref, gb_ref):
    mod = mod_ref[0, 0]
    h = _modulate(x_ref[0], g_ref[...], mod[0:1], mod[1:2]).astype(BF16)
    main = _dot(h, wm_ref[...])
    q_ref[0] = (main[:, 0:GLA_DK] * GLA_DK_HEAD ** -0.5).astype(BF16)
    k_ref[0] = main[:, GLA_DK:2 * GLA_DK].astype(BF16)
    v_ref[0] = main[:, 2 * GLA_DK:2 * GLA_DK + GLA_DV].astype(BF16)
    r_ref[0] = main[:, 2 * GLA_DK + GLA_DV:].astype(BF16)
    low = _dot(h, wl_ref[...])
    pre = _dot3(low, w2_ref[...]) + b2_ref[...]
    logsig = jnp.minimum(pre, 0.0) - jnp.log(1.0 + jnp.exp(-jnp.abs(pre)))
    gate = logsig / GLA_GATE_NORM
    gf_ref[0] = gate[:, 0:GLA_DK]
    gb_ref[0] = gate[:, GLA_DK:]


def _gla_project(xa, mod, g, wm, wl, w2, b2, nct):
    b, t, d = xa.shape
    nt = t // _TT
    seg = lambda j: jnp.where(j >= nct, 1, 0)
    tile = lambda n: pl.BlockSpec((1, _TT, n), lambda i, j: (i, j, 0))
    full = lambda a: pl.BlockSpec(a.shape, lambda i, j: (0,) * a.ndim)
    return pl.pallas_call(
        _gla_in_kernel,
        grid=(b, nt),
        in_specs=[
            tile(d),
            pl.BlockSpec((1, 1, N_MOD, d), lambda i, j: (i, seg(j), 0, 0)),
            full(g), full(wm), full(wl), full(w2), full(b2),
        ],
        out_specs=[tile(GLA_DK), tile(GLA_DK), tile(GLA_DV), tile(GLA_DV),
                   tile(GLA_DK), tile(GLA_DK)],
        out_shape=[
            jax.ShapeDtypeStruct((b, t, GLA_DK), BF16),
            jax.ShapeDtypeStruct((b, t, GLA_DK), BF16),
            jax.ShapeDtypeStruct((b, t, GLA_DV), BF16),
            jax.ShapeDtypeStruct((b, t, GLA_DV), BF16),
            jax.ShapeDtypeStruct((b, t, GLA_DK), F32),
            jax.ShapeDtypeStruct((b, t, GLA_DK), F32),
        ],
        compiler_params=_cparams("parallel", "parallel"),
        name="gla_in",
    )(xa, mod, g, wm, wl, w2, b2)


def _gla_scan_kernel(q_ref, k_ref, v_ref, g_ref, o_ref, st_ref, *, reverse):
    c = _GLA_CHUNK

    @pl.when(pl.program_id(2) == 0)
    def _():
        st_ref[...] = jnp.zeros_like(st_ref)

    row = lax.broadcasted_iota(jnp.int32, (c, c), 0)
    col = lax.broadcasted_iota(jnp.int32, (c, c), 1)
    causal = (col >= row) if reverse else (col <= row)
    tri = causal.astype(BF16)
    n_chunks = _TT // c
    order = range(n_chunks - 1, -1, -1) if reverse else range(n_chunks)
    for ci in order:
        rows = slice(ci * c, (ci + 1) * c)
        g = g_ref[0, rows, :]
        gh, gl = _split_bf16(g)
        bcum = _dot(tri, gh) + _dot(tri, gl)
        b_end = bcum[0:1] if reverse else bcum[c - 1:c]
        qd = (q_ref[0, rows, :].astype(F32) * jnp.exp(bcum)).astype(BF16)
        ke = k_ref[0, rows, :].astype(F32) * jnp.exp(-bcum)
        kd = (ke * jnp.exp(b_end)).astype(BF16)
        v = v_ref[0, rows, :]
        a = jnp.where(causal, _dot_nt(qd, ke.astype(BF16)), 0.0).astype(BF16)
        st = st_ref[...]
        o_ref[0, rows, :] = _dot_nt(qd, st.astype(BF16)) + _dot(a, v)
        vt = v.astype(F32).T.astype(BF16)
        st_ref[...] = st * jnp.exp(b_end) + _dot(vt, kd)


def _gla_scan(q, k, v, g, nct, reverse):
    b, t, _ = q.shape
    nb = t // _TT

    def blk(s):
        if not reverse:
            return s
        return jnp.where(s < nct, nct - 1 - s, nb - 1 - (s - nct))

    kspec = pl.BlockSpec((1, _TT, GLA_DK_HEAD), lambda i, h, s: (i, blk(s), h))
    vspec = pl.BlockSpec((1, _TT, GLA_DV_HEAD), lambda i, h, s: (i, blk(s), h))
    return pl.pallas_call(
        functools.partial(_gla_scan_kernel, reverse=reverse),
        grid=(b, GLA_HEADS, nb),
        in_specs=[kspec, kspec, vspec, kspec],
        out_specs=vspec,
        out_shape=jax.ShapeDtypeStruct((b, t, GLA_DV), F32),
        scratch_shapes=[pltpu.VMEM((GLA_DV_HEAD, GLA_DK_HEAD), F32)],
        compiler_params=_cparams("parallel", "parallel", "arbitrary"),
        name="gla_scan_bwd" if reverse else "gla_scan_fwd",
    )(q, k, v, g)


def _gla_out_kernel(of_ref, ob_ref, r_ref, og_ref, wo_ref, x_ref, mod_ref, o_ref):
    o = of_ref[0] + ob_ref[0]
    og = og_ref[...]
    parts = []
    for h in range(GLA_HEADS):
        oh = o[:, h * GLA_DV_HEAD:(h + 1) * GLA_DV_HEAD]
        ms = jnp.mean(oh * oh, axis=-1, keepdims=True)
        parts.append(oh * lax.rsqrt(ms + RMS_EPS) * og)
    r = r_ref[0].astype(F32)
    y = jnp.concatenate(parts, axis=1) * (r * _sigmoid(r))
    out = _dot(y.astype(BF16), wo_ref[...])
    o_ref[0] = x_ref[0] + mod_ref[0, 0][2:3] * out


def _gla_readout(o_f, o_b, r, og, wo, xa, mod, nct):
    b, t, d = xa.shape
    nt = t // _TT
    seg = lambda j: jnp.where(j >= nct, 1, 0)
    tile = lambda n: pl.BlockSpec((1, _TT, n), lambda i, j: (i, j, 0))
    return pl.pallas_call(
        _gla_out_kernel,
        grid=(b, nt),
        in_specs=[
            tile(GLA_DV), tile(GLA_DV), tile(GLA_DV),
            pl.BlockSpec((1, GLA_DV_HEAD), lambda i, j: (0, 0)),
            pl.BlockSpec((GLA_DV, d), lambda i, j: (0, 0)),
            tile(d),
            pl.BlockSpec((1, 1, N_MOD, d), lambda i, j: (i, seg(j), 0, 0)),
        ],
        out_specs=tile(d),
        out_shape=jax.ShapeDtypeStruct((b, t, d), F32),
        compiler_params=_cparams("parallel", "parallel"),
        name="gla_out",
    )(o_f, o_b, r, og, wo, xa, mod)


def _router_kernel(x_ref, mod_ref, g_ref, wr_ref, br_ref, h_ref, idx_ref, rank_ref, w_ref,
                   cnt_ref, carry_ref):
    i = pl.program_id(0)

    @pl.when(i == 0)
    def _():
        carry_ref[...] = jnp.zeros_like(carry_ref)

    mod = mod_ref[0, 0]
    h = _modulate(x_ref[...], g_ref[...], mod[3:4], mod[4:5])
    h_ref[...] = h
    logits = _dot3(h, wr_ref[...]) + br_ref[...]
    lt = logits.T[0:N_EXPERTS]
    eio = lax.broadcasted_iota(jnp.int32, lt.shape, 0)
    vals, hots = [], []
    sub = lax.broadcasted_iota(jnp.int32, (8, _TT), 0)
    idx_rows = jnp.zeros((8, _TT), jnp.int32)
    for kk in range(TOP_K):
        m = jnp.max(lt, axis=0, keepdims=True)
        ik = jnp.min(jnp.where(lt == m, eio, N_EXPERTS), axis=0, keepdims=True)
        hot = eio == ik
        lt = jnp.where(hot, -jnp.inf, lt)
        vals.append(m)
        hots.append(hot)
        idx_rows = jnp.where(sub == kk, ik, idx_rows)
    es = [jnp.exp(v - vals[0]) for v in vals]
    denom = es[0] + es[1] + es[2] + es[3]
    member = jnp.zeros(lt.shape, F32)
    for hot in hots:
        member = member + hot.astype(F32)
    tri_r = lax.broadcasted_iota(jnp.int32, (_TT, _TT), 0)
    tri_c = lax.broadcasted_iota(jnp.int32, (_TT, _TT), 1)
    before = (tri_r < tri_c).astype(BF16)
    carry = carry_ref[...]
    rank = _dot(member.astype(BF16), before) + carry[:, 0:1]
    rank_rows = jnp.zeros((8, _TT), jnp.int32)
    w_rows = jnp.zeros((_LANES, _TT), F32)
    wsub = lax.broadcasted_iota(jnp.int32, (_LANES, _TT), 0)
    for kk in range(TOP_K):
        rk = jnp.sum(jnp.where(hots[kk], rank, 0.0), axis=0, keepdims=True)
        rank_rows = jnp.where(sub == kk, rk.astype(jnp.int32), rank_rows)
        w_rows = jnp.where(wsub == kk, es[kk] / denom, w_rows)
    idx_ref[...] = idx_rows
    rank_ref[...] = rank_rows
    w_ref[...] = w_rows.T
    carry = carry + jnp.sum(member, axis=1, keepdims=True)
    carry_ref[...] = carry
    cnt_ref[...] = carry


def _router(xf, mod, g, wr, br, ntb, nct):
    n, d = xf.shape
    nt = n // _TT
    seg = lambda s: jnp.where(s % ntb >= nct, 1, 0)
    return pl.pallas_call(
        _router_kernel,
        grid=(nt,),
        in_specs=[
            pl.BlockSpec((_TT, d), lambda s: (s, 0)),
            pl.BlockSpec((1, 1, N_MOD, d), lambda s: (s // ntb, seg(s), 0, 0)),
            pl.BlockSpec((1, d), lambda s: (0, 0)),
            pl.BlockSpec((d, _LANES), lambda s: (0, 0)),
            pl.BlockSpec((1, _LANES), lambda s: (0, 0)),
        ],
        out_specs=[
            pl.BlockSpec((_TT, d), lambda s: (s, 0)),
            pl.BlockSpec((8, _TT), lambda s: (0, s)),
            pl.BlockSpec((8, _TT), lambda s: (0, s)),
            pl.BlockSpec((_TT, _LANES), lambda s: (s, 0)),
            pl.BlockSpec((N_EXPERTS, _LANES), lambda s: (0, 0)),
        ],
        out_shape=[
            jax.ShapeDtypeStruct((n, d), F32),
            jax.ShapeDtypeStruct((8, n), jnp.int32),
            jax.ShapeDtypeStruct((8, n), jnp.int32),
            jax.ShapeDtypeStruct((n, _LANES), F32),
            jax.ShapeDtypeStruct((N_EXPERTS, _LANES), F32),
        ],
        scratch_shapes=[pltpu.VMEM((N_EXPERTS, _LANES), F32)],
        compiler_params=_cparams("arbitrary"),
        name="moe_router",
    )(xf, mod, g, wr, br)


def _row_copy(src_ref, src_row, dst_ref, dst_row, sem):
    return pltpu.make_async_copy(src_ref.at[pl.ds(src_row, 1)], dst_ref.at[pl.ds(dst_row, 1)], sem)


def _dispatch_kernel(pos_ref, h_ref, xs_in_ref, xs_ref, sem):
    del xs_in_ref

    def issue(n, carry):
        for kk in range(TOP_K):
            _row_copy(h_ref, n, xs_ref, pos_ref[kk, n], sem).start()
        return carry

    def drain(n, carry):
        for kk in range(TOP_K):
            _row_copy(h_ref, 0, xs_ref, 0, sem).wait()
        return carry

    lax.fori_loop(0, _TT, issue, 0)
    lax.fori_loop(0, _TT, drain, 0)


def _dispatch(pos, h, xs0):
    n, d = h.shape
    nt = n // _TT
    return pl.pallas_call(
        _dispatch_kernel,
        grid=(nt,),
        in_specs=[
            pl.BlockSpec((8, _TT), lambda s: (0, s), memory_space=pltpu.SMEM),
            pl.BlockSpec((_TT, d), lambda s: (s, 0)),
            pl.BlockSpec(memory_space=pl.ANY),
        ],
        out_specs=pl.BlockSpec(memory_space=pl.ANY),
        out_shape=jax.ShapeDtypeStruct(xs0.shape, xs0.dtype),
        scratch_shapes=[pltpu.SemaphoreType.DMA],
        input_output_aliases={2: 0},
        compiler_params=_cparams("arbitrary"),
        name="moe_dispatch",
    )(pos, h, xs0)


def _expert_kernel(te_ref, nu_ref, xs_ref, wgu_ref, bgu_ref, wd_ref, bd_ref, ys_ref):
    t = pl.program_id(0)

    @pl.when(t < nu_ref[0])
    def _():
        x = xs_ref[...].astype(BF16)
        gu = _dot(x, wgu_ref[0]) + bgu_ref[0]
        gate = jnp.minimum(gu[:, 0:D_EXPERT], SWIGLU_LIMIT)
        up = jnp.clip(gu[:, D_EXPERT:], -SWIGLU_LIMIT, SWIGLU_LIMIT)
        act = (up + 1.0) * (gate * _sigmoid(SWIGLU_ALPHA * gate))
        ys_ref[...] = _dot(act.astype(BF16), wd_ref[0]) + bd_ref[0]

    @pl.when(t >= nu_ref[0])
    def _():
        ys_ref[...] = jnp.zeros_like(ys_ref)


def _experts(tile_expert, n_used, xs, wgu, bgu, wd, bd):
    p, d = xs.shape
    nt = p // _TM
    e, _, n2 = wgu.shape
    grid_spec = pltpu.PrefetchScalarGridSpec(
        num_scalar_prefetch=2,
        grid=(nt,),
        in_specs=[
            pl.BlockSpec((_TM, d), lambda t, te, nu: (jnp.minimum(t, nu[0] - 1), 0)),
            pl.BlockSpec((1, d, n2), lambda t, te, nu: (te[t], 0, 0)),
            pl.BlockSpec((1, 1, n2), lambda t, te, nu: (te[t], 0, 0)),
            pl.BlockSpec((1, D_EXPERT, d), lambda t, te, nu: (te[t], 0, 0)),
            pl.BlockSpec((1, 1, d), lambda t, te, nu: (te[t], 0, 0)),
        ],
        out_specs=pl.BlockSpec((_TM, d), lambda t, te, nu: (t, 0)),
    )
    return pl.pallas_call(
        _expert_kernel,
        grid_spec=grid_spec,
        out_shape=jax.ShapeDtypeStruct((p, d), F32),
        compiler_params=_cparams("arbitrary"),
        name="moe_experts",
    )(tile_expert, n_used, xs, wgu, bgu.reshape(e, 1, n2), wd, bd.reshape(e, 1, d))


def _combine_kernel(pos_ref, ys_ref, w_ref, x_ref, mod_ref, o_ref, buf_ref, sem):
    def issue(n, carry):
        for kk in range(TOP_K):
            _row_copy(ys_ref, pos_ref[kk, n], buf_ref.at[kk], n, sem).start()
        return carry

    def drain(n, carry):
        for kk in range(TOP_K):
            _row_copy(ys_ref, 0, buf_ref.at[kk], 0, sem).wait()
        return carry

    lax.fori_loop(0, _TT, issue, 0)
    lax.fori_loop(0, _TT, drain, 0)
    w = w_ref[...]
    f = w[:, 0:1] * buf_ref[0]
    for kk in range(1, TOP_K):
        f = f + w[:, kk:kk + 1] * buf_ref[kk]
    o_ref[...] = x_ref[...] + mod_ref[0, 0][5:6] * f


def _combine(pos, ys, w, xf, mod, ntb, nct):
    n, d = xf.shape
    nt = n // _TT
    seg = lambda s: jnp.where(s % ntb >= nct, 1, 0)
    return pl.pallas_call(
        _combine_kernel,
        grid=(nt,),
        in_specs=[
            pl.BlockSpec((8, _TT), lambda s: (0, s), memory_space=pltpu.SMEM),
            pl.BlockSpec(memory_space=pl.ANY),
            pl.BlockSpec((_TT, _LANES), lambda s: (s, 0)),
            pl.BlockSpec((_TT, d), lambda s: (s, 0)),
            pl.BlockSpec((1, 1, N_MOD, d), lambda s: (s // ntb, seg(s), 0, 0)),
        ],
        out_specs=pl.BlockSpec((_TT, d), lambda s: (s, 0)),
        out_shape=jax.ShapeDtypeStruct((n, d), F32),
        scratch_shapes=[pltpu.VMEM((TOP_K, _TT, d), F32), pltpu.SemaphoreType.DMA],
        compiler_params=_cparams("arbitrary"),
        name="moe_combine",
    )(pos, ys, w, xf, mod)


def _moe(xa, mod, g2, wr, br, wgu, bgu, wd, bd, nct):
    b, t, d = xa.shape
    n = b * t
    ntb = t // _TT
    xf = xa.reshape(n, d)
    h, idx, rank, w, cnt = _router(xf, mod, g2, wr, br, ntb, nct)
    counts = cnt[:, 0].astype(jnp.int32)
    padded = (counts + _TM - 1) // _TM * _TM
    ends = jnp.cumsum(padded)
    pos = (ends - padded)[idx] + rank
    n_tiles = (n * TOP_K + N_EXPERTS * (_TM - 1)) // _TM
    tile_expert = jnp.minimum(
        jnp.searchsorted(ends, jnp.arange(n_tiles, dtype=jnp.int32) * _TM, side="right"),
        N_EXPERTS - 1).astype(jnp.int32)
    n_used = (ends[-1:] // _TM).astype(jnp.int32)
    xs = _dispatch(pos, h, jnp.zeros((n_tiles * _TM, d), F32))
    ys = _experts(tile_expert, n_used, xs, wgu, bgu, wd, bd)
    return _combine(pos, ys, w, xf, mod, ntb, nct).reshape(b, t, d)


def _rope_tables(n_ctx, n_lat):
    rows = n_lat // GRID_W
    row_ids = jnp.repeat(jnp.arange(rows), GRID_W).astype(F32)
    col_ids = jnp.tile(jnp.arange(GRID_W), rows).astype(F32)
    axis_dim = HEAD_DIM // 2
    inv_freq = 1.0 / (ROPE_THETA ** (jnp.arange(0, axis_dim, 2, dtype=F32) / axis_dim))
    ang = jnp.concatenate([row_ids[:, None] * inv_freq, col_ids[:, None] * inv_freq], axis=-1)
    cos = jnp.concatenate([jnp.ones((n_ctx, axis_dim), F32), jnp.cos(ang)], axis=0)
    sin = jnp.concatenate([jnp.zeros((n_ctx, axis_dim), F32), jnp.sin(ang)], axis=0)
    return cos.T, sin.T


def kernel(x, c, ctx, c_ctx, ada_w, ada_b, norm1_g, norm2_g, attn_w_qkv, attn_q_gain, attn_k_gain, attn_w_o, gla_w_in, gla_w_gk2_f, gla_b_gk_f, gla_w_gk2_b, gla_b_gk_b, gla_o_gain, gla_w_o, moe_w_router, moe_b_router, moe_w_gu, moe_b_gu, moe_w_down, moe_b_down):
    b, n_lat, d = x.shape
    n_ctx = ctx.shape[1]
    assert d == D_MODEL and n_ctx % _TT == 0 and n_lat % _TT == 0 and n_lat % GRID_W == 0
    nct = n_ctx // _TT
    depth = ada_w.shape[0]

    r = -(-(b + 1) // 8) * 8
    cond = jnp.zeros((r, d), F32).at[:b].set(c).at[b].set(c_ctx)
    table = _ada_table(cond, ada_w, ada_b).reshape(depth, r, N_MOD, d)
    mods = jnp.stack([jnp.broadcast_to(table[:, b:b + 1], (depth, b, N_MOD, d)), table[:, :b]],
                     axis=2)

    cos_t, sin_t = _rope_tables(n_ctx, n_lat)
    perm = jnp.concatenate([jnp.arange(0, HEAD_DIM, 2), jnp.arange(1, HEAD_DIM, 2)])
    qk_cols = (jnp.arange(N_Q_HEADS + N_KV_HEADS)[:, None] * HEAD_DIM + perm[None, :]).reshape(-1)
    cols = jnp.concatenate([qk_cols, jnp.arange(Q_DIM + KV_DIM, QKV_DIM)])

    xa = jnp.concatenate([ctx, x], axis=1)
    for i in range(depth):
        mod = mods[i]
        j = i // N_MIXERS
        g1 = norm1_g[i].reshape(1, d)
        if i % N_MIXERS == 0:
            wt = attn_w_qkv[j][:, cols].T.astype(BF16)
            gq = attn_q_gain[j][perm].reshape(HEAD_DIM, 1)
            gk = attn_k_gain[j][perm].reshape(HEAD_DIM, 1)
            qt, k, vt = _qkv_project(xa, mod, g1, wt, gq, gk, cos_t, sin_t, nct)
            xa = _attention(qt, k, vt, attn_w_o[j].astype(BF16), xa, mod, nct)
        else:
            n_main = 2 * GLA_DK + 2 * GLA_DV
            wm = gla_w_in[j][:, :n_main].astype(BF16)
            wl = jnp.zeros((d, _LANES), F32).at[:, :2 * GLA_GATE_RANK].set(
                gla_w_in[j][:, n_main:]).astype(BF16)
            w2 = jnp.zeros((_LANES, 2 * GLA_DK), F32)
            w2 = w2.at[:GLA_GATE_RANK, :GLA_DK].set(gla_w_gk2_f[j])
            w2 = w2.at[GLA_GATE_RANK:2 * GLA_GATE_RANK, GLA_DK:].set(gla_w_gk2_b[j])
            b2 = jnp.concatenate([gla_b_gk_f[j], gla_b_gk_b[j]]).reshape(1, 2 * GLA_DK)
            q, k, v, rr, gf, gb = _gla_project(xa, mod, g1, wm, wl, w2, b2, nct)
            o_f = _gla_scan(q, k, v, gf, nct, reverse=False)
            o_b = _gla_scan(q, k, v, gb, nct, reverse=True)
            xa = _gla_readout(o_f, o_b, rr, gla_o_gain[j].reshape(1, GLA_DV_HEAD),
                              gla_w_o[j].astype(BF16), xa, mod, nct)
        wr = jnp.zeros((d, _LANES), F32).at[:, :N_EXPERTS].set(moe_w_router[i])
        br = jnp.full((1, _LANES), _NEG, F32).at[0, :N_EXPERTS].set(moe_b_router[i])
        xa = _moe(xa, mod, norm2_g[i].reshape(1, d), wr, br,
                  moe_w_gu[i].astype(BF16), moe_b_gu[i], moe_w_down[i].astype(BF16),
                  moe_b_down[i], nct)
    return xa[:, n_ctx:]
```

```python
import functools

import jax
import jax.numpy as jnp
from jax import lax
from jax.experimental import pallas as pl
from jax.experimental.pallas import tpu as pltpu

F32 = jnp.float32
BF16 = jnp.bfloat16

D_MODEL = 1024
DEPTH = 4
GRID_W = 64
N_MIXERS = 2
N_MOD = 6
RMS_EPS = 1e-6
HEAD_DIM = 64
N_Q_HEADS = D_MODEL // HEAD_DIM
N_KV_HEADS = N_Q_HEADS // 4
GQA_GROUP = N_Q_HEADS // N_KV_HEADS
Q_DIM = N_Q_HEADS * HEAD_DIM
KV_DIM = N_KV_HEADS * HEAD_DIM
QKV_DIM = Q_DIM + 2 * KV_DIM
V_ROWS = HEAD_DIM + 16
LOG2E = 1.4426950408889634
ROPE_THETA = 10000.0
GLA_HEADS = 4
GLA_DK = D_MODEL // 2
GLA_DV = D_MODEL
GLA_DK_HEAD = GLA_DK // GLA_HEADS
GLA_DV_HEAD = GLA_DV // GLA_HEADS
GLA_GATE_RANK = 16
GLA_GATE_NORM = 16.0
N_EXPERTS = 32
TOP_K = 4
D_EXPERT = D_MODEL
SWIGLU_LIMIT = 7.0
SWIGLU_ALPHA = 1.702

_LANES = 128
_TT = 256
_KC = 256
_GLA_SAFE_LOG_DECAY = 60.0
_TM = 256
_ADA_TN = 1536
_VMEM_LIMIT = 56 * 1024 * 1024
_NEG = -1e30


def _cparams(*sem):
    return pltpu.CompilerParams(dimension_semantics=sem, vmem_limit_bytes=_VMEM_LIMIT)


def _dot(a, b):
    return jnp.dot(a, b, preferred_element_type=F32)


def _dot_nt(a, b):
    return lax.dot_general(a, b, (((1,), (1,)), ((), ())), preferred_element_type=F32)


def _split_bf16(a):
    hi = a.astype(BF16)
    lo = (a - hi.astype(F32)).astype(BF16)
    return hi, lo


def _dot3(a, w):
    ah, al = _split_bf16(a)
    wh, wl = _split_bf16(w)
    return _dot(ah, wh) + _dot(ah, wl) + _dot(al, wh)


def _modulate(x, g, shift, scale):
    ms = jnp.mean(x * x, axis=-1, keepdims=True)
    y = x * lax.rsqrt(ms + RMS_EPS) * g
    return y * (1.0 + scale) + shift


def _sigmoid(x):
    return 1.0 / (1.0 + jnp.exp(-x))


def _ada_kernel(cond_ref, w_ref, b_ref, o_ref):
    c = cond_ref[...]
    o_ref[0] = _dot3(c * _sigmoid(c), w_ref[0]) + b_ref[0]


def _ada_table(cond, ada_w, ada_b):
    depth, d, n = ada_w.shape
    r = cond.shape[0]
    return pl.pallas_call(
        _ada_kernel,
        grid=(depth, n // _ADA_TN),
        in_specs=[
            pl.BlockSpec((r, d), lambda i, j: (0, 0)),
            pl.BlockSpec((1, d, _ADA_TN), lambda i, j: (i, 0, j)),
            pl.BlockSpec((1, 1, _ADA_TN), lambda i, j: (i, 0, j)),
        ],
        out_specs=pl.BlockSpec((1, r, _ADA_TN), lambda i, j: (i, 0, j)),
        out_shape=jax.ShapeDtypeStruct((depth, r, n), F32),
        compiler_params=_cparams("parallel", "parallel"),
        name="ada_table",
    )(cond, ada_w, ada_b.reshape(depth, 1, n))


def _qkv_kernel(x_ref, mod_ref, g_ref, wt_ref, gq_ref, gk_ref, cos_ref, sin_ref,
                qt_ref, k_ref, vt_ref):
    mod = mod_ref[0, 0]
    h = _modulate(x_ref[0], g_ref[...], mod[0:1], mod[1:2])
    ht = h.T.astype(BF16)
    qkvt = _dot(wt_ref[...], ht)
    c = cos_ref[...]
    s = sin_ref[...]
    half = HEAD_DIM // 2

    def norm_rope(blk, gain, scale):
        ms = jnp.mean(blk * blk, axis=0, keepdims=True)
        n = blk * lax.rsqrt(ms + RMS_EPS) * gain
        x1 = n[0:half]
        x2 = n[half:HEAD_DIM]
        return jnp.concatenate([x1 * c - x2 * s, x1 * s + x2 * c], axis=0) * scale

    gq = gq_ref[...]
    gk = gk_ref[...]
    for hh in range(N_Q_HEADS):
        r0 = hh * HEAD_DIM
        qt_ref[0, r0:r0 + HEAD_DIM, :] = norm_rope(
            qkvt[r0:r0 + HEAD_DIM], gq, LOG2E * HEAD_DIM ** -0.5).astype(BF16)
    kt = jnp.concatenate(
        [norm_rope(qkvt[Q_DIM + j * HEAD_DIM:Q_DIM + (j + 1) * HEAD_DIM], gk, 1.0)
         for j in range(N_KV_HEADS)], axis=0)
    k_ref[0] = kt.T.astype(BF16)
    ones = jnp.ones((V_ROWS - HEAD_DIM, vt_ref.shape[2]), BF16)
    for j in range(N_KV_HEADS):
        v0 = Q_DIM + KV_DIM + j * HEAD_DIM
        vt_ref[0, j * V_ROWS:j * V_ROWS + HEAD_DIM, :] = qkvt[v0:v0 + HEAD_DIM].astype(BF16)
        vt_ref[0, j * V_ROWS + HEAD_DIM:(j + 1) * V_ROWS, :] = ones


def _qkv_project(xa, mod, g, wt, gq, gk, cos_t, sin_t, nct):
    b, t, d = xa.shape
    nt = t // _TT
    seg = lambda j: jnp.where(j >= nct, 1, 0)
    return pl.pallas_call(
        _qkv_kernel,
        grid=(b, nt),
        in_specs=[
            pl.BlockSpec((1, _TT, d), lambda i, j: (i, j, 0)),
            pl.BlockSpec((1, 1, N_MOD, d), lambda i, j: (i, seg(j), 0, 0)),
            pl.BlockSpec((1, d), lambda i, j: (0, 0)),
            pl.BlockSpec((QKV_DIM, d), lambda i, j: (0, 0)),
            pl.BlockSpec((HEAD_DIM, 1), lambda i, j: (0, 0)),
            pl.BlockSpec((HEAD_DIM, 1), lambda i, j: (0, 0)),
            pl.BlockSpec((HEAD_DIM // 2, _TT), lambda i, j: (0, j)),
            pl.BlockSpec((HEAD_DIM // 2, _TT), lambda i, j: (0, j)),
        ],
        out_specs=[
            pl.BlockSpec((1, Q_DIM, _TT), lambda i, j: (i, 0, j)),
            pl.BlockSpec((1, _TT, KV_DIM), lambda i, j: (i, j, 0)),
            pl.BlockSpec((1, N_KV_HEADS * V_ROWS, _TT), lambda i, j: (i, 0, j)),
        ],
        out_shape=[
            jax.ShapeDtypeStruct((b, Q_DIM, t), BF16),
            jax.ShapeDtypeStruct((b, t, KV_DIM), BF16),
            jax.ShapeDtypeStruct((b, N_KV_HEADS * V_ROWS, t), BF16),
        ],
        compiler_params=_cparams("parallel", "parallel"),
        name="attn_qkv",
    )(xa, mod, g, wt, gq, gk, cos_t, sin_t)


def _attn_kernel(qt_ref, k_ref, vt_ref, wo_ref, x_ref, mod_ref, o_ref, qe_ref, ot_ref,
                 *, nct, n_ctx):
    j = pl.program_id(1)
    t_all = k_ref.shape[1]

    def heads(nk):
        def head_pair(hp, carry):
            kvh = hp // (GQA_GROUP // 2)
            k0 = pl.multiple_of(kvh * HEAD_DIM, HEAD_DIM)
            v0 = pl.multiple_of(kvh * V_ROWS, 16)
            for u in range(2):
                q0 = pl.multiple_of((2 * hp + u) * HEAD_DIM, HEAD_DIM)
                qe_ref[u] = jnp.zeros(qe_ref.shape[1:], BF16)
                qe_ref[u, pl.ds(k0, HEAD_DIM), :] = qt_ref[0, pl.ds(q0, HEAD_DIM), :]
            def scores(c):
                return [_dot(k_ref[0, c * _KC:(c + 1) * _KC, :], qe_ref[u]) for u in range(2)]

            acc = [jnp.zeros((V_ROWS, _TT), F32) for _ in range(2)]
            m_run = [jnp.full((1, _TT), _NEG, F32) for _ in range(2)]
            s_next = scores(0)
            for c in range(nk // _KC):
                s_cur = s_next
                if (c + 1) * _KC < nk:
                    s_next = scores(c + 1)
                for u in range(2):
                    m_new = jnp.maximum(m_run[u], jnp.max(s_cur[u], axis=0, keepdims=True))
                    p = jnp.exp2(s_cur[u] - m_new).astype(BF16)
                    pv = _dot(vt_ref[0, pl.ds(v0, V_ROWS), c * _KC:(c + 1) * _KC], p)
                    acc[u] = acc[u] * jnp.exp2(m_run[u] - m_new) + pv
                    m_run[u] = m_new
            for u in range(2):
                q0 = pl.multiple_of((2 * hp + u) * HEAD_DIM, HEAD_DIM)
                ot_ref[pl.ds(q0, HEAD_DIM), :] = acc[u][0:HEAD_DIM] / acc[u][HEAD_DIM:HEAD_DIM + 1]
            return carry

        lax.fori_loop(0, N_Q_HEADS // 2, head_pair, 0)

    @pl.when(j < nct)
    def _():
        heads(n_ctx)

    @pl.when(j >= nct)
    def _():
        heads(t_all)

    att = ot_ref[...].T.astype(BF16)
    out = _dot(att, wo_ref[...])
    o_ref[0] = x_ref[0] + mod_ref[0, 0][2:3] * out


def _attention(qt, k, vt, wo, xa, mod, nct):
    b, t, d = xa.shape
    nt = t // _TT
    seg = lambda j: jnp.where(j >= nct, 1, 0)
    kern = functools.partial(_attn_kernel, nct=nct, n_ctx=nct * _TT)
    return pl.pallas_call(
        kern,
        grid=(b, nt),
        in_specs=[
            pl.BlockSpec((1, Q_DIM, _TT), lambda i, j: (i, 0, j)),
            pl.BlockSpec((1, t, KV_DIM), lambda i, j: (i, 0, 0)),
            pl.BlockSpec((1, N_KV_HEADS * V_ROWS, t), lambda i, j: (i, 0, 0)),
            pl.BlockSpec((Q_DIM, d), lambda i, j: (0, 0)),
            pl.BlockSpec((1, _TT, d), lambda i, j: (i, j, 0)),
            pl.BlockSpec((1, 1, N_MOD, d), lambda i, j: (i, seg(j), 0, 0)),
        ],
        out_specs=pl.BlockSpec((1, _TT, d), lambda i, j: (i, j, 0)),
        out_shape=jax.ShapeDtypeStruct((b, t, d), F32),
        scratch_shapes=[pltpu.VMEM((2, KV_DIM, _TT), BF16), pltpu.VMEM((Q_DIM, _TT), F32)],
        compiler_params=_cparams("parallel", "arbitrary"),
        name="attn_core",
    )(qt, k, vt, wo, xa, mod)


def _gla_in_kernel(x_ref, mod_ref, g_ref, wm_ref, wl_ref, w2_ref, b2_ref,
                   q_ref, k_ref, v_ref, r_ref, gf_ref, gb_ref):
    mod = mod_ref[0, 0]
    h = _modulate(x_ref[0], g_ref[...], mod[0:1], mod[1:2]).astype(BF16)
    main = _dot(h, wm_ref[...])
    q_ref[0] = (main[:, 0:GLA_DK] * GLA_DK_HEAD ** -0.5).astype(BF16)
    k_ref[0] = main[:, GLA_DK:2 * GLA_DK].astype(BF16)
    v_ref[0] = main[:, 2 * GLA_DK:2 * GLA_DK + GLA_DV].astype(BF16)
    r_ref[0] = main[:, 2 * GLA_DK + GLA_DV:].astype(BF16)
    low = _dot(h, wl_ref[...])
    pre = _dot3(low, w2_ref[...]) + b2_ref[...]
    logsig = jnp.minimum(pre, 0.0) - jnp.log(1.0 + jnp.exp(-jnp.abs(pre)))
    gate = logsig / GLA_GATE_NORM
    gf_ref[0] = gate[:, 0:GLA_DK]
    gb_ref[0] = gate[:, GLA_DK:]


def _gla_project(xa, mod, g, wm, wl, w2, b2, nct):
    b, t, d = xa.shape
    nt = t // _TT
    seg = lambda j: jnp.where(j >= nct, 1, 0)
    tile = lambda n: pl.BlockSpec((1, _TT, n), lambda i, j: (i, j, 0))
    full = lambda a: pl.BlockSpec(a.shape, lambda i, j: (0,) * a.ndim)
    return pl.pallas_call(
        _gla_in_kernel,
        grid=(b, nt),
        in_specs=[
            tile(d),
            pl.BlockSpec((1, 1, N_MOD, d), lambda i, j: (i, seg(j), 0, 0)),
            full(g), full(wm), full(wl), full(w2), full(b2),
        ],
        out_specs=[tile(GLA_DK), tile(GLA_DK), tile(GLA_DV), tile(GLA_DV),
                   tile(GLA_DK), tile(GLA_DK)],
        out_shape=[
            jax.ShapeDtypeStruct((b, t, GLA_DK), BF16),
            jax.ShapeDtypeStruct((b, t, GLA_DK), BF16),
            jax.ShapeDtypeStruct((b, t, GLA_DV), BF16),
            jax.ShapeDtypeStruct((b, t, GLA_DV), BF16),
            jax.ShapeDtypeStruct((b, t, GLA_DK), F32),
            jax.ShapeDtypeStruct((b, t, GLA_DK), F32),
        ],
        compiler_params=_cparams("parallel", "parallel"),
        name="gla_in",
    )(xa, mod, g, wm, wl, w2, b2)


def _gla_chunk(q_ref, k_ref, v_ref, g_ref, tri_ref, o_ref, st_ref, a_ref, kf_ref, bf_ref, reverse):
    tri = tri_ref[...]
    causal = tri > 0
    gh, gl = _split_bf16(g_ref[0])
    bcum = _dot(tri, gh) + _dot(tri, gl)
    b_end = bcum[0:1] if reverse else bcum[_TT - 1:_TT]
    q = q_ref[0].astype(F32)
    k = k_ref[0].astype(F32)
    v = v_ref[0]
    qd = (q * jnp.exp(bcum)).astype(BF16)
    kd = (k * jnp.exp(b_end - bcum)).astype(BF16)
    a_ref[...] = _dot_nt(qd, (k * jnp.exp(-bcum)).astype(BF16))

    @pl.when(jnp.max(-b_end) > _GLA_SAFE_LOG_DECAY)
    def _():
        kf_ref[...] = k
        bf_ref[...] = bcum
        lane = lax.broadcasted_iota(jnp.int32, (_TT, _TT), 1)

        def column(j, carry):
            decay = jnp.exp(jnp.minimum(bcum - bf_ref[pl.ds(j, 1), :], 0.0))
            col = jnp.sum(q * decay * kf_ref[pl.ds(j, 1), :], axis=1, keepdims=True)
            a_ref[...] = jnp.where(lane == j, col, a_ref[...])
            return carry

        lax.fori_loop(0, _TT, column, 0)

    a = jnp.where(causal, a_ref[...], 0.0).astype(BF16)
    st = st_ref[...]
    o_ref[0] = _dot_nt(qd, st.astype(BF16)) + _dot(a, v)
    vt = v.astype(F32).T.astype(BF16)
    st_ref[...] = st * jnp.exp(b_end) + _dot(vt, kd)


def _gla_scan_kernel(qf_ref, kf_ref, vf_ref, gf_ref, qb_ref, kb_ref, vb_ref, gb_ref,
                     tril_ref, triu_ref, of_ref, ob_ref,
                     sf_ref, sb_ref, af_ref, ab_ref, kx_ref, bx_ref):
    @pl.when(pl.program_id(2) == 0)
    def _():
        sf_ref[...] = jnp.zeros_like(sf_ref)
        sb_ref[...] = jnp.zeros_like(sb_ref)

    _gla_chunk(qf_ref, kf_ref, vf_ref, gf_ref, tril_ref, of_ref, sf_ref, af_ref, kx_ref, bx_ref,
               reverse=False)
    _gla_chunk(qb_ref, kb_ref, vb_ref, gb_ref, triu_ref, ob_ref, sb_ref, ab_ref, kx_ref, bx_ref,
               reverse=True)


def _gla_scan(q, k, v, gf, gb, nct):
    b, t, _ = q.shape
    nb = t // _TT

    def rev(s):
        return jnp.where(s < nct, nct - 1 - s, nb - 1 - (s - nct))

    kspec = pl.BlockSpec((1, _TT, GLA_DK_HEAD), lambda i, h, s: (i, s, h))
    vspec = pl.BlockSpec((1, _TT, GLA_DV_HEAD), lambda i, h, s: (i, s, h))
    kspec_r = pl.BlockSpec((1, _TT, GLA_DK_HEAD), lambda i, h, s: (i, rev(s), h))
    vspec_r = pl.BlockSpec((1, _TT, GLA_DV_HEAD), lambda i, h, s: (i, rev(s), h))
    tspec = pl.BlockSpec((_TT, _TT), lambda i, h, s: (0, 0))
    row = lax.broadcasted_iota(jnp.int32, (_TT, _TT), 0)
    col = lax.broadcasted_iota(jnp.int32, (_TT, _TT), 1)
    return pl.pallas_call(
        _gla_scan_kernel,
        grid=(b, GLA_HEADS, nb),
        in_specs=[kspec, kspec, vspec, kspec, kspec_r, kspec_r, vspec_r, kspec_r, tspec, tspec],
        out_specs=[vspec, vspec_r],
        out_shape=[jax.ShapeDtypeStruct((b, t, GLA_DV), F32)] * 2,
        scratch_shapes=[pltpu.VMEM((GLA_DV_HEAD, GLA_DK_HEAD), F32)] * 2
        + [pltpu.VMEM((_TT, _TT), F32)] * 2
        + [pltpu.VMEM((_TT, GLA_DK_HEAD), F32)] * 2,
        compiler_params=_cparams("parallel", "parallel", "arbitrary"),
        name="gla_scan",
    )(q, k, v, gf, q, k, v, gb, (col <= row).astype(BF16), (col >= row).astype(BF16))


def _gla_out_kernel(of_ref, ob_ref, r_ref, og_ref, wo_ref, x_ref, mod_ref, o_ref):
    o = of_ref[0] + ob_ref[0]
    og = og_ref[...]
    parts = []
    for h in range(GLA_HEADS):
        oh = o[:, h * GLA_DV_HEAD:(h + 1) * GLA_DV_HEAD]
        ms = jnp.mean(oh * oh, axis=-1, keepdims=True)
        parts.append(oh * lax.rsqrt(ms + RMS_EPS) * og)
    r = r_ref[0].astype(F32)
    y = jnp.concatenate(parts, axis=1) * (r * _sigmoid(r))
    out = _dot(y.astype(BF16), wo_ref[...])
    o_ref[0] = x_ref[0] + mod_ref[0, 0][2:3] * out


def _gla_readout(o_f, o_b, r, og, wo, xa, mod, nct):
    b, t, d = xa.shape
    nt = t // _TT
    seg = lambda j: jnp.where(j >= nct, 1, 0)
    tile = lambda n: pl.BlockSpec((1, _TT, n), lambda i, j: (i, j, 0))
    return pl.pallas_call(
        _gla_out_kernel,
        grid=(b, nt),
        in_specs=[
            tile(GLA_DV), tile(GLA_DV), tile(GLA_DV),
            pl.BlockSpec((1, GLA_DV_HEAD), lambda i, j: (0, 0)),
            pl.BlockSpec((GLA_DV, d), lambda i, j: (0, 0)),
            tile(d),
            pl.BlockSpec((1, 1, N_MOD, d), lambda i, j: (i, seg(j), 0, 0)),
        ],
        out_specs=tile(d),
        out_shape=jax.ShapeDtypeStruct((b, t, d), F32),
        compiler_params=_cparams("parallel", "parallel"),
        name="gla_out",
    )(o_f, o_b, r, og, wo, xa, mod)


def _router_kernel(x_ref, mod_ref, g_ref, wr_ref, br_ref, h_ref, idx_ref, rank_ref, w_ref,
                   cnt_ref, carry_ref):
    i = pl.program_id(0)

    @pl.when(i == 0)
    def _():
        carry_ref[...] = jnp.zeros_like(carry_ref)

    mod = mod_ref[0, 0]
    h = _modulate(x_ref[...], g_ref[...], mod[3:4], mod[4:5])
    h_ref[...] = h
    logits = _dot3(h, wr_ref[...]) + br_ref[...]
    lt = logits.T[0:N_EXPERTS]
    eio = lax.broadcasted_iota(jnp.int32, lt.shape, 0)
    vals, hots = [], []
    sub = lax.broadcasted_iota(jnp.int32, (8, _TT), 0)
    idx_rows = jnp.zeros((8, _TT), jnp.int32)
    for kk in range(TOP_K):
        m = jnp.max(lt, axis=0, keepdims=True)
        ik = jnp.min(jnp.where(lt == m, eio, N_EXPERTS), axis=0, keepdims=True)
        hot = eio == ik
        lt = jnp.where(hot, -jnp.inf, lt)
        vals.append(m)
        hots.append(hot)
        idx_rows = jnp.where(sub == kk, ik, idx_rows)
    es = [jnp.exp(v - vals[0]) for v in vals]
    denom = es[0] + es[1] + es[2] + es[3]
    member = jnp.zeros(lt.shape, F32)
    for hot in hots:
        member = member + hot.astype(F32)
    tri_r = lax.broadcasted_iota(jnp.int32, (_TT, _TT), 0)
    tri_c = lax.broadcasted_iota(jnp.int32, (_TT, _TT), 1)
    before = (tri_r < tri_c).astype(BF16)
    carry = carry_ref[...]
    rank = _dot(member.astype(BF16), before) + carry[:, 0:1]
    rank_rows = jnp.zeros((8, _TT), jnp.int32)
    w_rows = jnp.zeros((_LANES, _TT), F32)
    wsub = lax.broadcasted_iota(jnp.int32, (_LANES, _TT), 0)
    for kk in range(TOP_K):
        rk = jnp.sum(jnp.where(hots[kk], rank, 0.0), axis=0, keepdims=True)
        rank_rows = jnp.where(sub == kk, rk.astype(jnp.int32), rank_rows)
        w_rows = jnp.where(wsub == kk, es[kk] / denom, w_rows)
    idx_ref[...] = idx_rows
    rank_ref[...] = rank_rows
    w_ref[...] = w_rows.T
    carry = carry + jnp.sum(member, axis=1, keepdims=True)
    carry_ref[...] = carry
    cnt_ref[...] = carry


def _router(xf, mod, g, wr, br, ntb, nct):
    n, d = xf.shape
    nt = n // _TT
    seg = lambda s: jnp.where(s % ntb >= nct, 1, 0)
    return pl.pallas_call(
        _router_kernel,
        grid=(nt,),
        in_specs=[
            pl.BlockSpec((_TT, d), lambda s: (s, 0)),
            pl.BlockSpec((1, 1, N_MOD, d), lambda s: (s // ntb, seg(s), 0, 0)),
            pl.BlockSpec((1, d), lambda s: (0, 0)),
            pl.BlockSpec((d, _LANES), lambda s: (0, 0)),
            pl.BlockSpec((1, _LANES), lambda s: (0, 0)),
        ],
        out_specs=[
            pl.BlockSpec((_TT, d), lambda s: (s, 0)),
            pl.BlockSpec((8, _TT), lambda s: (0, s)),
            pl.BlockSpec((8, _TT), lambda s: (0, s)),
            pl.BlockSpec((_TT, _LANES), lambda s: (s, 0)),
            pl.BlockSpec((N_EXPERTS, _LANES), lambda s: (0, 0)),
        ],
        out_shape=[
            jax.ShapeDtypeStruct((n, d), F32),
            jax.ShapeDtypeStruct((8, n), jnp.int32),
            jax.ShapeDtypeStruct((8, n), jnp.int32),
            jax.ShapeDtypeStruct((n, _LANES), F32),
            jax.ShapeDtypeStruct((N_EXPERTS, _LANES), F32),
        ],
        scratch_shapes=[pltpu.VMEM((N_EXPERTS, _LANES), F32)],
        compiler_params=_cparams("arbitrary"),
        name="moe_router",
    )(xf, mod, g, wr, br)


def _row_copy(src_ref, src_row, dst_ref, dst_row, sem):
    return pltpu.make_async_copy(src_ref.at[pl.ds(src_row, 1)], dst_ref.at[pl.ds(dst_row, 1)], sem)


def _dispatch_kernel(pos_ref, h_ref, xs_in_ref, xs_ref, sem):
    del xs_in_ref

    def issue(n, carry):
        for kk in range(TOP_K):
            _row_copy(h_ref, n, xs_ref, pos_ref[kk, n], sem).start()
        return carry

    lax.fori_loop(0, _TT, issue, 0, unroll=8)
    for kk in range(TOP_K):
        pltpu.make_async_copy(h_ref, xs_ref.at[pl.ds(0, _TT)], sem).wait()


def _dispatch(pos, h, xs0):
    n, d = h.shape
    nt = n // _TT
    return pl.pallas_call(
        _dispatch_kernel,
        grid=(nt,),
        in_specs=[
            pl.BlockSpec((8, _TT), lambda s: (0, s), memory_space=pltpu.SMEM),
            pl.BlockSpec((_TT, d), lambda s: (s, 0)),
            pl.BlockSpec(memory_space=pl.ANY),
        ],
        out_specs=pl.BlockSpec(memory_space=pl.ANY),
        out_shape=jax.ShapeDtypeStruct(xs0.shape, xs0.dtype),
        scratch_shapes=[pltpu.SemaphoreType.DMA],
        input_output_aliases={2: 0},
        compiler_params=_cparams("arbitrary"),
        name="moe_dispatch",
    )(pos, h, xs0)


def _expert_kernel(te_ref, nu_ref, xs_ref, wgu_ref, bgu_ref, wd_ref, bd_ref, ys_ref,
                   wgu_bf_ref, wd_bf_ref):
    t = pl.program_id(0)
    live = t < nu_ref[0]
    new_expert = jnp.logical_or(t == 0, te_ref[t] != te_ref[jnp.maximum(t - 1, 0)])

    @pl.when(jnp.logical_and(live, new_expert))
    def _():
        wgu_bf_ref[...] = wgu_ref[0, 0].astype(BF16)
        wd_bf_ref[...] = wd_ref[0, 0].astype(BF16)

    @pl.when(live)
    def _():
        x = xs_ref[...].astype(BF16)
        gu = _dot(x, wgu_bf_ref[...]) + bgu_ref[0, 0]
        gate = jnp.minimum(gu[:, 0:D_EXPERT], SWIGLU_LIMIT)
        up = jnp.clip(gu[:, D_EXPERT:], -SWIGLU_LIMIT, SWIGLU_LIMIT)
        act = (up + 1.0) * (gate * _sigmoid(SWIGLU_ALPHA * gate))
        ys_ref[...] = _dot(act.astype(BF16), wd_bf_ref[...]) + bd_ref[0, 0]

    @pl.when(jnp.logical_not(live))
    def _():
        ys_ref[...] = jnp.zeros_like(ys_ref)


def _experts(tile_expert, n_used, xs, layer, wgu, bgu, wd, bd):
    p, d = xs.shape
    nt = p // _TM
    depth, e, _, n2 = wgu.shape
    grid_spec = pltpu.PrefetchScalarGridSpec(
        num_scalar_prefetch=2,
        grid=(nt,),
        in_specs=[
            pl.BlockSpec((_TM, d), lambda t, te, nu: (jnp.minimum(t, nu[0] - 1), 0)),
            pl.BlockSpec((1, 1, d, n2), lambda t, te, nu: (layer, te[t], 0, 0)),
            pl.BlockSpec((1, 1, 1, n2), lambda t, te, nu: (layer, te[t], 0, 0)),
            pl.BlockSpec((1, 1, D_EXPERT, d), lambda t, te, nu: (layer, te[t], 0, 0)),
            pl.BlockSpec((1, 1, 1, d), lambda t, te, nu: (layer, te[t], 0, 0)),
        ],
        out_specs=pl.BlockSpec((_TM, d), lambda t, te, nu: (t, 0)),
        scratch_shapes=[pltpu.VMEM((d, n2), BF16), pltpu.VMEM((D_EXPERT, d), BF16)],
    )
    return pl.pallas_call(
        _expert_kernel,
        grid_spec=grid_spec,
        out_shape=jax.ShapeDtypeStruct((p, d), F32),
        compiler_params=_cparams("arbitrary"),
        name="moe_experts",
    )(tile_expert, n_used, xs, wgu, bgu.reshape(depth, e, 1, n2), wd, bd.reshape(depth, e, 1, d))


def _combine_kernel(pos_ref, ys_ref, w_ref, x_ref, mod_ref, o_ref, buf_ref, sem):
    def issue(n, carry):
        for kk in range(TOP_K):
            _row_copy(ys_ref, pos_ref[kk, n], buf_ref.at[kk], n, sem).start()
        return carry

    lax.fori_loop(0, _TT, issue, 0, unroll=8)
    for kk in range(TOP_K):
        pltpu.make_async_copy(ys_ref.at[pl.ds(0, _TT)], buf_ref.at[kk], sem).wait()
    w = w_ref[...]
    f = w[:, 0:1] * buf_ref[0]
    for kk in range(1, TOP_K):
        f = f + w[:, kk:kk + 1] * buf_ref[kk]
    o_ref[...] = x_ref[...] + mod_ref[0, 0][5:6] * f


def _combine(pos, ys, w, xf, mod, ntb, nct):
    n, d = xf.shape
    nt = n // _TT
    seg = lambda s: jnp.where(s % ntb >= nct, 1, 0)
    return pl.pallas_call(
        _combine_kernel,
        grid=(nt,),
        in_specs=[
            pl.BlockSpec((8, _TT), lambda s: (0, s), memory_space=pltpu.SMEM),
            pl.BlockSpec(memory_space=pl.ANY),
            pl.BlockSpec((_TT, _LANES), lambda s: (s, 0)),
            pl.BlockSpec((_TT, d), lambda s: (s, 0)),
            pl.BlockSpec((1, 1, N_MOD, d), lambda s: (s // ntb, seg(s), 0, 0)),
        ],
        out_specs=pl.BlockSpec((_TT, d), lambda s: (s, 0)),
        out_shape=jax.ShapeDtypeStruct((n, d), F32),
        scratch_shapes=[pltpu.VMEM((TOP_K, _TT, d), F32), pltpu.SemaphoreType.DMA],
        compiler_params=_cparams("arbitrary"),
        name="moe_combine",
    )(pos, ys, w, xf, mod)


def _moe(xa, mod, g2, wr, br, layer, wgu, bgu, wd, bd, nct):
    b, t, d = xa.shape
    n = b * t
    ntb = t // _TT
    xf = xa.reshape(n, d)
    h, idx, rank, w, cnt = _router(xf, mod, g2, wr, br, ntb, nct)
    counts = cnt[:, 0].astype(jnp.int32)
    padded = (counts + _TM - 1) // _TM * _TM
    ends = jnp.cumsum(padded)
    starts = ends - padded
    first = jnp.sum(jnp.where(idx[:, :, None] == jnp.arange(N_EXPERTS), starts, 0), axis=-1)
    pos = first + rank
    n_tiles = (n * TOP_K + N_EXPERTS * (_TM - 1)) // _TM
    tile_row = jnp.arange(n_tiles, dtype=jnp.int32) * _TM
    tile_expert = jnp.minimum(jnp.sum(tile_row[:, None] >= ends[None, :], axis=1),
                              N_EXPERTS - 1).astype(jnp.int32)
    n_used = (ends[-1:] // _TM).astype(jnp.int32)
    xs = _dispatch(pos, h, jnp.zeros((n_tiles * _TM, d), F32))
    ys = _experts(tile_expert, n_used, xs, layer, wgu, bgu, wd, bd)
    return _combine(pos, ys, w, xf, mod, ntb, nct).reshape(b, t, d)


def _rope_tables(n_ctx, n_lat):
    rows = n_lat // GRID_W
    row_ids = jnp.repeat(jnp.arange(rows), GRID_W).astype(F32)
    col_ids = jnp.tile(jnp.arange(GRID_W), rows).astype(F32)
    axis_dim = HEAD_DIM // 2
    inv_freq = 1.0 / (ROPE_THETA ** (jnp.arange(0, axis_dim, 2, dtype=F32) / axis_dim))
    ang = jnp.concatenate([row_ids[:, None] * inv_freq, col_ids[:, None] * inv_freq], axis=-1)
    cos = jnp.concatenate([jnp.ones((n_ctx, axis_dim), F32), jnp.cos(ang)], axis=0)
    sin = jnp.concatenate([jnp.zeros((n_ctx, axis_dim), F32), jnp.sin(ang)], axis=0)
    return cos.T, sin.T


def kernel(x, c, ctx, c_ctx, ada_w, ada_b, norm1_g, norm2_g, attn_w_qkv, attn_q_gain, attn_k_gain, attn_w_o, gla_w_in, gla_w_gk2_f, gla_b_gk_f, gla_w_gk2_b, gla_b_gk_b, gla_o_gain, gla_w_o, moe_w_router, moe_b_router, moe_w_gu, moe_b_gu, moe_w_down, moe_b_down):
    b, n_lat, d = x.shape
    n_ctx = ctx.shape[1]
    assert d == D_MODEL and n_ctx % _TT == 0 and n_lat % _TT == 0 and n_lat % GRID_W == 0
    nct = n_ctx // _TT
    depth = ada_w.shape[0]

    r = -(-(b + 1) // 8) * 8
    cond = jnp.zeros((r, d), F32).at[:b].set(c).at[b].set(c_ctx)
    table = _ada_table(cond, ada_w, ada_b).reshape(depth, r, N_MOD, d)
    mods = jnp.stack([jnp.broadcast_to(table[:, b:b + 1], (depth, b, N_MOD, d)), table[:, :b]],
                     axis=2)

    cos_t, sin_t = _rope_tables(n_ctx, n_lat)
    perm = jnp.concatenate([jnp.arange(0, HEAD_DIM, 2), jnp.arange(1, HEAD_DIM, 2)])
    qk_cols = (jnp.arange(N_Q_HEADS + N_KV_HEADS)[:, None] * HEAD_DIM + perm[None, :]).reshape(-1)
    cols = jnp.concatenate([qk_cols, jnp.arange(Q_DIM + KV_DIM, QKV_DIM)])

    xa = jnp.concatenate([ctx, x], axis=1)
    for i in range(depth):
        mod = mods[i]
        j = i // N_MIXERS
        g1 = norm1_g[i].reshape(1, d)
        if i % N_MIXERS == 0:
            wt = attn_w_qkv[j][:, cols].T.astype(BF16)
            gq = attn_q_gain[j][perm].reshape(HEAD_DIM, 1)
            gk = attn_k_gain[j][perm].reshape(HEAD_DIM, 1)
            qt, k, vt = _qkv_project(xa, mod, g1, wt, gq, gk, cos_t, sin_t, nct)
            xa = _attention(qt, k, vt, attn_w_o[j].astype(BF16), xa, mod, nct)
        else:
            n_main = 2 * GLA_DK + 2 * GLA_DV
            wm = gla_w_in[j][:, :n_main].astype(BF16)
            wl = jnp.zeros((d, _LANES), F32).at[:, :2 * GLA_GATE_RANK].set(
                gla_w_in[j][:, n_main:]).astype(BF16)
            w2 = jnp.zeros((_LANES, 2 * GLA_DK), F32)
            w2 = w2.at[:GLA_GATE_RANK, :GLA_DK].set(gla_w_gk2_f[j])
            w2 = w2.at[GLA_GATE_RANK:2 * GLA_GATE_RANK, GLA_DK:].set(gla_w_gk2_b[j])
            b2 = jnp.concatenate([gla_b_gk_f[j], gla_b_gk_b[j]]).reshape(1, 2 * GLA_DK)
            q, k, v, rr, gf, gb = _gla_project(xa, mod, g1, wm, wl, w2, b2, nct)
            o_f, o_b = _gla_scan(q, k, v, gf, gb, nct)
            xa = _gla_readout(o_f, o_b, rr, gla_o_gain[j].reshape(1, GLA_DV_HEAD),
                              gla_w_o[j].astype(BF16), xa, mod, nct)
        wr = jnp.zeros((d, _LANES), F32).at[:, :N_EXPERTS].set(moe_w_router[i])
        br = jnp.full((1, _LANES), _NEG, F32).at[0, :N_EXPERTS].set(moe_b_router[i])
        xa = _moe(xa, mod, norm2_g[i].reshape(1, d), wr, br,
                  i, moe_w_gu, moe_b_gu, moe_w_down, moe_b_down, nct)
    return xa[:, n_ctx:]
```

```python
import functools

import jax
import jax.numpy as jnp
from jax import lax
from jax.experimental import pallas as pl
from jax.experimental.pallas import tpu as pltpu

F32 = jnp.float32
BF16 = jnp.bfloat16

D_MODEL = 1024
DEPTH = 4
GRID_W = 64
N_MIXERS = 2
N_MOD = 6
RMS_EPS = 1e-6
HEAD_DIM = 64
N_Q_HEADS = D_MODEL // HEAD_DIM
N_KV_HEADS = N_Q_HEADS // 4
GQA_GROUP = N_Q_HEADS // N_KV_HEADS
Q_DIM = N_Q_HEADS * HEAD_DIM
KV_DIM = N_KV_HEADS * HEAD_DIM
QKV_DIM = Q_DIM + 2 * KV_DIM
V_ROWS = HEAD_DIM + 16
LOG2E = 1.4426950408889634
ROPE_THETA = 10000.0
GLA_HEADS = 4
GLA_DK = D_MODEL // 2
GLA_DV = D_MODEL
GLA_DK_HEAD = GLA_DK // GLA_HEADS
GLA_DV_HEAD = GLA_DV // GLA_HEADS
GLA_GATE_RANK = 16
GLA_GATE_NORM = 16.0
N_EXPERTS = 32
TOP_K = 4
D_EXPERT = D_MODEL
SWIGLU_LIMIT = 7.0
SWIGLU_ALPHA = 1.702

_LANES = 128
_TT = 256
_KC = 512
_GLA_SAFE_LOG_DECAY = 60.0
_TM = 256
_RUN = 8
_SLOTS = _TT * TOP_K + N_EXPERTS * _RUN
_ADA_TN = 1536
_VMEM_LIMIT = 56 * 1024 * 1024
_NEG = -1e30


def _cparams(*sem):
    return pltpu.CompilerParams(dimension_semantics=sem, vmem_limit_bytes=_VMEM_LIMIT)


def _dot(a, b):
    return jnp.dot(a, b, preferred_element_type=F32)


def _dot_nt(a, b):
    return lax.dot_general(a, b, (((1,), (1,)), ((), ())), preferred_element_type=F32)


def _split_bf16(a):
    hi = a.astype(BF16)
    lo = (a - hi.astype(F32)).astype(BF16)
    return hi, lo


def _dot3(a, w):
    ah, al = _split_bf16(a)
    wh, wl = _split_bf16(w)
    return _dot(ah, wh) + _dot(ah, wl) + _dot(al, wh)


def _modulate(x, g, shift, scale):
    ms = jnp.mean(x * x, axis=-1, keepdims=True)
    y = x * lax.rsqrt(ms + RMS_EPS) * g
    return y * (1.0 + scale) + shift


def _sigmoid(x):
    return 1.0 / (1.0 + jnp.exp(-x))


def _ada_kernel(cond_ref, w_ref, b_ref, o_ref):
    c = cond_ref[...]
    o_ref[0] = _dot3(c * _sigmoid(c), w_ref[0]) + b_ref[0]


def _ada_table(cond, ada_w, ada_b):
    depth, d, n = ada_w.shape
    r = cond.shape[0]
    return pl.pallas_call(
        _ada_kernel,
        grid=(depth, n // _ADA_TN),
        in_specs=[
            pl.BlockSpec((r, d), lambda i, j: (0, 0)),
            pl.BlockSpec((1, d, _ADA_TN), lambda i, j: (i, 0, j)),
            pl.BlockSpec((1, 1, _ADA_TN), lambda i, j: (i, 0, j)),
        ],
        out_specs=pl.BlockSpec((1, r, _ADA_TN), lambda i, j: (i, 0, j)),
        out_shape=jax.ShapeDtypeStruct((depth, r, n), F32),
        compiler_params=_cparams("parallel", "parallel"),
        name="ada_table",
    )(cond, ada_w, ada_b.reshape(depth, 1, n))


def _qkv_kernel(x_ref, mod_ref, g_ref, wt_ref, gq_ref, gk_ref, cos_ref, sin_ref,
                qt_ref, k_ref, vt_ref):
    mod = mod_ref[0, 0]
    h = _modulate(x_ref[0], g_ref[...], mod[0:1], mod[1:2])
    ht = h.T.astype(BF16)
    qkvt = _dot(wt_ref[...], ht)
    c = cos_ref[...]
    s = sin_ref[...]
    half = HEAD_DIM // 2

    def norm_rope(blk, gain, scale):
        ms = jnp.mean(blk * blk, axis=0, keepdims=True)
        n = blk * lax.rsqrt(ms + RMS_EPS) * gain
        x1 = n[0:half]
        x2 = n[half:HEAD_DIM]
        return jnp.concatenate([x1 * c - x2 * s, x1 * s + x2 * c], axis=0) * scale

    gq = gq_ref[...]
    gk = gk_ref[...]
    for hh in range(N_Q_HEADS):
        r0 = hh * HEAD_DIM
        qt_ref[0, r0:r0 + HEAD_DIM, :] = norm_rope(
            qkvt[r0:r0 + HEAD_DIM], gq, LOG2E * HEAD_DIM ** -0.5).astype(BF16)
    kt = jnp.concatenate(
        [norm_rope(qkvt[Q_DIM + j * HEAD_DIM:Q_DIM + (j + 1) * HEAD_DIM], gk, 1.0)
         for j in range(N_KV_HEADS)], axis=0)
    k_ref[0] = kt.T.astype(BF16)
    ones = jnp.ones((V_ROWS - HEAD_DIM, vt_ref.shape[2]), BF16)
    for j in range(N_KV_HEADS):
        v0 = Q_DIM + KV_DIM + j * HEAD_DIM
        vt_ref[0, j * V_ROWS:j * V_ROWS + HEAD_DIM, :] = qkvt[v0:v0 + HEAD_DIM].astype(BF16)
        vt_ref[0, j * V_ROWS + HEAD_DIM:(j + 1) * V_ROWS, :] = ones


def _qkv_project(xa, mod, g, wt, gq, gk, cos_t, sin_t, nct):
    b, t, d = xa.shape
    nt = t // _TT
    seg = lambda j: jnp.where(j >= nct, 1, 0)
    return pl.pallas_call(
        _qkv_kernel,
        grid=(b, nt),
        in_specs=[
            pl.BlockSpec((1, _TT, d), lambda i, j: (i, j, 0)),
            pl.BlockSpec((1, 1, N_MOD, d), lambda i, j: (i, seg(j), 0, 0)),
            pl.BlockSpec((1, d), lambda i, j: (0, 0)),
            pl.BlockSpec((QKV_DIM, d), lambda i, j: (0, 0)),
            pl.BlockSpec((HEAD_DIM, 1), lambda i, j: (0, 0)),
            pl.BlockSpec((HEAD_DIM, 1), lambda i, j: (0, 0)),
            pl.BlockSpec((HEAD_DIM // 2, _TT), lambda i, j: (0, j)),
            pl.BlockSpec((HEAD_DIM // 2, _TT), lambda i, j: (0, j)),
        ],
        out_specs=[
            pl.BlockSpec((1, Q_DIM, _TT), lambda i, j: (i, 0, j)),
            pl.BlockSpec((1, _TT, KV_DIM), lambda i, j: (i, j, 0)),
            pl.BlockSpec((1, N_KV_HEADS * V_ROWS, _TT), lambda i, j: (i, 0, j)),
        ],
        out_shape=[
            jax.ShapeDtypeStruct((b, Q_DIM, t), BF16),
            jax.ShapeDtypeStruct((b, t, KV_DIM), BF16),
            jax.ShapeDtypeStruct((b, N_KV_HEADS * V_ROWS, t), BF16),
        ],
        compiler_params=_cparams("parallel", "parallel"),
        name="attn_qkv",
    )(xa, mod, g, wt, gq, gk, cos_t, sin_t)


def _attn_kernel(qt_ref, k_ref, vt_ref, wo_ref, x_ref, mod_ref, o_ref, qe_ref, ot_ref,
                 *, nct, n_ctx):
    j = pl.program_id(1)
    t_all = k_ref.shape[1]

    def heads(nk):
        def head_pair(hp, carry):
            kvh = hp // (GQA_GROUP // 2)
            k0 = pl.multiple_of(kvh * HEAD_DIM, HEAD_DIM)
            v0 = pl.multiple_of(kvh * V_ROWS, 16)
            for u in range(2):
                q0 = pl.multiple_of((2 * hp + u) * HEAD_DIM, HEAD_DIM)
                qe_ref[u] = jnp.zeros(qe_ref.shape[1:], BF16)
                qe_ref[u, pl.ds(k0, HEAD_DIM), :] = qt_ref[0, pl.ds(q0, HEAD_DIM), :]
            bounds = list(range(0, nk, _KC)) + [nk]
            chunks = list(zip(bounds[:-1], bounds[1:]))

            def scores(c):
                lo, hi = chunks[c]
                return [_dot(k_ref[0, lo:hi, :], qe_ref[u]) for u in range(2)]

            acc = [jnp.zeros((V_ROWS, _TT), F32) for _ in range(2)]
            m_run = [jnp.full((1, _TT), _NEG, F32) for _ in range(2)]
            s_next = scores(0)
            for c, (lo, hi) in enumerate(chunks):
                s_cur = s_next
                if c + 1 < len(chunks):
                    s_next = scores(c + 1)
                for u in range(2):
                    m_new = jnp.maximum(m_run[u], jnp.max(s_cur[u], axis=0, keepdims=True))
                    p = jnp.exp2(s_cur[u] - m_new).astype(BF16)
                    pv = _dot(vt_ref[0, pl.ds(v0, V_ROWS), lo:hi], p)
                    acc[u] = acc[u] * jnp.exp2(m_run[u] - m_new) + pv
                    m_run[u] = m_new
            for u in range(2):
                q0 = pl.multiple_of((2 * hp + u) * HEAD_DIM, HEAD_DIM)
                ot_ref[pl.ds(q0, HEAD_DIM), :] = acc[u][0:HEAD_DIM] / acc[u][HEAD_DIM:HEAD_DIM + 1]
            return carry

        lax.fori_loop(0, N_Q_HEADS // 2, head_pair, 0)

    @pl.when(j < nct)
    def _():
        heads(n_ctx)

    @pl.when(j >= nct)
    def _():
        heads(t_all)

    att = ot_ref[...].T.astype(BF16)
    out = _dot(att, wo_ref[...])
    o_ref[0] = x_ref[0] + mod_ref[0, 0][2:3] * out


def _attention(qt, k, vt, wo, xa, mod, nct):
    b, t, d = xa.shape
    nt = t // _TT
    seg = lambda j: jnp.where(j >= nct, 1, 0)
    kern = functools.partial(_attn_kernel, nct=nct, n_ctx=nct * _TT)
    return pl.pallas_call(
        kern,
        grid=(b, nt),
        in_specs=[
            pl.BlockSpec((1, Q_DIM, _TT), lambda i, j: (i, 0, j)),
            pl.BlockSpec((1, t, KV_DIM), lambda i, j: (i, 0, 0)),
            pl.BlockSpec((1, N_KV_HEADS * V_ROWS, t), lambda i, j: (i, 0, 0)),
            pl.BlockSpec((Q_DIM, d), lambda i, j: (0, 0)),
            pl.BlockSpec((1, _TT, d), lambda i, j: (i, j, 0)),
            pl.BlockSpec((1, 1, N_MOD, d), lambda i, j: (i, seg(j), 0, 0)),
        ],
        out_specs=pl.BlockSpec((1, _TT, d), lambda i, j: (i, j, 0)),
        out_shape=jax.ShapeDtypeStruct((b, t, d), F32),
        scratch_shapes=[pltpu.VMEM((2, KV_DIM, _TT), BF16), pltpu.VMEM((Q_DIM, _TT), F32)],
        compiler_params=_cparams("parallel", "arbitrary"),
        name="attn_core",
    )(qt, k, vt, wo, xa, mod)


def _gla_in_kernel(x_ref, mod_ref, g_ref, wm_ref, wl_ref, w2_ref, b2_ref,
                   q_ref, k_ref, v_ref, r_ref, gf_ref, gb_ref):
    mod = mod_ref[0, 0]
    h = _modulate(x_ref[0], g_ref[...], mod[0:1], mod[1:2]).astype(BF16)
    main = _dot(h, wm_ref[...])
    q_ref[0] = (main[:, 0:GLA_DK] * GLA_DK_HEAD ** -0.5).astype(BF16)
    k_ref[0] = main[:, GLA_DK:2 * GLA_DK].astype(BF16)
    v_ref[0] = main[:, 2 * GLA_DK:2 * GLA_DK + GLA_DV].astype(BF16)
    r_ref[0] = main[:, 2 * GLA_DK + GLA_DV:].astype(BF16)
    low = _dot(h, wl_ref[...])
    pre = _dot3(low, w2_ref[...]) + b2_ref[...]
    logsig = jnp.minimum(pre, 0.0) - jnp.log(1.0 + jnp.exp(-jnp.abs(pre)))
    gate = logsig / GLA_GATE_NORM
    gf_ref[0] = gate[:, 0:GLA_DK]
    gb_ref[0] = gate[:, GLA_DK:]


def _gla_project(xa, mod, g, wm, wl, w2, b2, nct):
    b, t, d = xa.shape
    nt = t // _TT
    seg = lambda j: jnp.where(j >= nct, 1, 0)
    tile = lambda n: pl.BlockSpec((1, _TT, n), lambda i, j: (i, j, 0))
    full = lambda a: pl.BlockSpec(a.shape, lambda i, j: (0,) * a.ndim)
    return pl.pallas_call(
        _gla_in_kernel,
        grid=(b, nt),
        in_specs=[
            tile(d),
            pl.BlockSpec((1, 1, N_MOD, d), lambda i, j: (i, seg(j), 0, 0)),
            full(g), full(wm), full(wl), full(w2), full(b2),
        ],
        out_specs=[tile(GLA_DK), tile(GLA_DK), tile(GLA_DV), tile(GLA_DV),
                   tile(GLA_DK), tile(GLA_DK)],
        out_shape=[
            jax.ShapeDtypeStruct((b, t, GLA_DK), BF16),
            jax.ShapeDtypeStruct((b, t, GLA_DK), BF16),
            jax.ShapeDtypeStruct((b, t, GLA_DV), BF16),
            jax.ShapeDtypeStruct((b, t, GLA_DV), BF16),
            jax.ShapeDtypeStruct((b, t, GLA_DK), F32),
            jax.ShapeDtypeStruct((b, t, GLA_DK), F32),
        ],
        compiler_params=_cparams("parallel", "parallel"),
        name="gla_in",
    )(xa, mod, g, wm, wl, w2, b2)


def _gla_chunk(q_ref, k_ref, v_ref, g_ref, tri_ref, o_ref, st_ref, a_ref, kf_ref, bf_ref, reverse):
    tri = tri_ref[...]
    causal = tri > 0
    gh, gl = _split_bf16(g_ref[0])
    bcum = _dot(tri, gh) + _dot(tri, gl)
    b_end = bcum[0:1] if reverse else bcum[_TT - 1:_TT]
    q = q_ref[0].astype(F32)
    k = k_ref[0].astype(F32)
    v = v_ref[0]
    qd = (q * jnp.exp(bcum)).astype(BF16)
    kd = (k * jnp.exp(b_end - bcum)).astype(BF16)
    a_ref[...] = _dot_nt(qd, (k * jnp.exp(-bcum)).astype(BF16))

    @pl.when(jnp.max(-b_end) > _GLA_SAFE_LOG_DECAY)
    def _():
        kf_ref[...] = k
        bf_ref[...] = bcum
        lane = lax.broadcasted_iota(jnp.int32, (_TT, _TT), 1)

        def column(j, carry):
            decay = jnp.exp(jnp.minimum(bcum - bf_ref[pl.ds(j, 1), :], 0.0))
            col = jnp.sum(q * decay * kf_ref[pl.ds(j, 1), :], axis=1, keepdims=True)
            a_ref[...] = jnp.where(lane == j, col, a_ref[...])
            return carry

        lax.fori_loop(0, _TT, column, 0)

    a = jnp.where(causal, a_ref[...], 0.0).astype(BF16)
    st = st_ref[...]
    o_ref[0] = _dot_nt(qd, st.astype(BF16)) + _dot(a, v)
    vt = v.astype(F32).T.astype(BF16)
    st_ref[...] = st * jnp.exp(b_end) + _dot(vt, kd)


def _gla_scan_kernel(qf_ref, kf_ref, vf_ref, gf_ref, qb_ref, kb_ref, vb_ref, gb_ref,
                     tril_ref, triu_ref, of_ref, ob_ref,
                     sf_ref, sb_ref, af_ref, ab_ref, kx_ref, bx_ref):
    @pl.when(pl.program_id(2) == 0)
    def _():
        sf_ref[...] = jnp.zeros_like(sf_ref)
        sb_ref[...] = jnp.zeros_like(sb_ref)

    _gla_chunk(qf_ref, kf_ref, vf_ref, gf_ref, tril_ref, of_ref, sf_ref, af_ref, kx_ref, bx_ref,
               reverse=False)
    _gla_chunk(qb_ref, kb_ref, vb_ref, gb_ref, triu_ref, ob_ref, sb_ref, ab_ref, kx_ref, bx_ref,
               reverse=True)


def _gla_scan(q, k, v, gf, gb, nct):
    b, t, _ = q.shape
    nb = t // _TT

    def rev(s):
        return jnp.where(s < nct, nct - 1 - s, nb - 1 - (s - nct))

    kspec = pl.BlockSpec((1, _TT, GLA_DK_HEAD), lambda i, h, s: (i, s, h))
    vspec = pl.BlockSpec((1, _TT, GLA_DV_HEAD), lambda i, h, s: (i, s, h))
    kspec_r = pl.BlockSpec((1, _TT, GLA_DK_HEAD), lambda i, h, s: (i, rev(s), h))
    vspec_r = pl.BlockSpec((1, _TT, GLA_DV_HEAD), lambda i, h, s: (i, rev(s), h))
    tspec = pl.BlockSpec((_TT, _TT), lambda i, h, s: (0, 0))
    row = lax.broadcasted_iota(jnp.int32, (_TT, _TT), 0)
    col = lax.broadcasted_iota(jnp.int32, (_TT, _TT), 1)
    return pl.pallas_call(
        _gla_scan_kernel,
        grid=(b, GLA_HEADS, nb),
        in_specs=[kspec, kspec, vspec, kspec, kspec_r, kspec_r, vspec_r, kspec_r, tspec, tspec],
        out_specs=[vspec, vspec_r],
        out_shape=[jax.ShapeDtypeStruct((b, t, GLA_DV), F32)] * 2,
        scratch_shapes=[pltpu.VMEM((GLA_DV_HEAD, GLA_DK_HEAD), F32)] * 2
        + [pltpu.VMEM((_TT, _TT), F32)] * 2
        + [pltpu.VMEM((_TT, GLA_DK_HEAD), F32)] * 2,
        compiler_params=_cparams("parallel", "parallel", "arbitrary"),
        name="gla_scan",
    )(q, k, v, gf, q, k, v, gb, (col <= row).astype(BF16), (col >= row).astype(BF16))


def _gla_out_kernel(of_ref, ob_ref, r_ref, og_ref, wo_ref, x_ref, mod_ref, o_ref):
    o = of_ref[0] + ob_ref[0]
    og = og_ref[...]
    parts = []
    for h in range(GLA_HEADS):
        oh = o[:, h * GLA_DV_HEAD:(h + 1) * GLA_DV_HEAD]
        ms = jnp.mean(oh * oh, axis=-1, keepdims=True)
        parts.append(oh * lax.rsqrt(ms + RMS_EPS) * og)
    r = r_ref[0].astype(F32)
    y = jnp.concatenate(parts, axis=1) * (r * _sigmoid(r))
    out = _dot(y.astype(BF16), wo_ref[...])
    o_ref[0] = x_ref[0] + mod_ref[0, 0][2:3] * out


def _gla_readout(o_f, o_b, r, og, wo, xa, mod, nct):
    b, t, d = xa.shape
    nt = t // _TT
    seg = lambda j: jnp.where(j >= nct, 1, 0)
    tile = lambda n: pl.BlockSpec((1, _TT, n), lambda i, j: (i, j, 0))
    return pl.pallas_call(
        _gla_out_kernel,
        grid=(b, nt),
        in_specs=[
            tile(GLA_DV), tile(GLA_DV), tile(GLA_DV),
            pl.BlockSpec((1, GLA_DV_HEAD), lambda i, j: (0, 0)),
            pl.BlockSpec((GLA_DV, d), lambda i, j: (0, 0)),
            tile(d),
            pl.BlockSpec((1, 1, N_MOD, d), lambda i, j: (i, seg(j), 0, 0)),
        ],
        out_specs=tile(d),
        out_shape=jax.ShapeDtypeStruct((b, t, d), F32),
        compiler_params=_cparams("parallel", "parallel"),
        name="gla_out",
    )(o_f, o_b, r, og, wo, xa, mod)


def _pad_run(count):
    return jnp.floor((count + (_RUN - 1)) * (1.0 / _RUN)) * _RUN


def _router_kernel(x_ref, mod_ref, g_ref, wr_ref, br_ref, h_ref, slot_ref, slotc_ref, w_ref,
                   tab_ref, tot_ref, carry_ref):
    i = pl.program_id(0)

    @pl.when(i == 0)
    def _():
        carry_ref[...] = jnp.zeros_like(carry_ref)

    mod = mod_ref[0, 0]
    h = _modulate(x_ref[...], g_ref[...], mod[3:4], mod[4:5])
    h_ref[...] = h
    logits = _dot3(h, wr_ref[...]) + br_ref[...]
    lt = logits.T[0:N_EXPERTS]
    eio = lax.broadcasted_iota(jnp.int32, lt.shape, 0)
    vals, hots = [], []
    for kk in range(TOP_K):
        m = jnp.max(lt, axis=0, keepdims=True)
        ik = jnp.min(jnp.where(lt == m, eio, N_EXPERTS), axis=0, keepdims=True)
        hot = eio == ik
        lt = jnp.where(hot, -jnp.inf, lt)
        vals.append(m)
        hots.append(hot)
    es = [jnp.exp(v - vals[0]) for v in vals]
    denom = es[0] + es[1] + es[2] + es[3]
    member = jnp.zeros(lt.shape, F32)
    for hot in hots:
        member = member + hot.astype(F32)
    member_bf = member.astype(BF16)

    r_t = lax.broadcasted_iota(jnp.int32, (_TT, _TT), 0)
    c_t = lax.broadcasted_iota(jnp.int32, (_TT, _TT), 1)
    before = (r_t < c_t).astype(BF16)
    in_run = _dot(member_bf, before)
    run_col = _pad_run(jnp.sum(member, axis=1, keepdims=True))
    r_e = lax.broadcasted_iota(jnp.int32, (N_EXPERTS, N_EXPERTS), 0)
    c_e = lax.broadcasted_iota(jnp.int32, (N_EXPERTS, N_EXPERTS), 1)
    first_col = _dot((c_e < r_e).astype(BF16),
                     jnp.broadcast_to(run_col, (N_EXPERTS, _LANES)).astype(BF16))[:, 0:1]
    slot = first_col + in_run
    rsub = lax.broadcasted_iota(jnp.int32, (_LANES, _TT), 0)
    slot_rows = jnp.zeros((_LANES, _TT), F32)
    w_rows = jnp.zeros((_LANES, _TT), F32)
    for kk in range(TOP_K):
        sk = jnp.sum(jnp.where(hots[kk], slot, 0.0), axis=0, keepdims=True)
        slot_rows = jnp.where(rsub == kk, sk, slot_rows)
        w_rows = jnp.where(rsub == kk, es[kk] / denom, w_rows)
    slot_ref[...] = slot_rows[0:8].astype(jnp.int32)
    slotc_ref[...] = slot_rows.T.astype(jnp.int32)
    w_ref[...] = w_rows.T

    member_pad = jnp.concatenate(
        [member_bf, jnp.zeros((_LANES - N_EXPERTS, _TT), BF16)], axis=0)
    run_row = _pad_run(_dot_nt(jnp.ones((8, _TT), BF16), member_pad))
    r_l = lax.broadcasted_iota(jnp.int32, (_LANES, _LANES), 0)
    c_l = lax.broadcasted_iota(jnp.int32, (_LANES, _LANES), 1)
    first_row = _dot(run_row.astype(BF16), (r_l < c_l).astype(BF16))
    earlier = carry_ref[...]
    tsub = lax.broadcasted_iota(jnp.int32, (8, _LANES), 0)
    tab = jnp.where(tsub == 0, run_row * (1.0 / _RUN),
                    jnp.where(tsub == 1, first_row, jnp.where(tsub == 2, earlier, 0.0)))
    tab_ref[0] = tab.astype(jnp.int32)
    carry_ref[...] = earlier + run_row
    tot_ref[...] = earlier + run_row


def _router(xf, mod, g, wr, br, ntb, nct):
    n, d = xf.shape
    nt = n // _TT
    seg = lambda s: jnp.where(s % ntb >= nct, 1, 0)
    return pl.pallas_call(
        _router_kernel,
        grid=(nt,),
        in_specs=[
            pl.BlockSpec((_TT, d), lambda s: (s, 0)),
            pl.BlockSpec((1, 1, N_MOD, d), lambda s: (s // ntb, seg(s), 0, 0)),
            pl.BlockSpec((1, d), lambda s: (0, 0)),
            pl.BlockSpec((d, _LANES), lambda s: (0, 0)),
            pl.BlockSpec((1, _LANES), lambda s: (0, 0)),
        ],
        out_specs=[
            pl.BlockSpec((_TT, d), lambda s: (s, 0)),
            pl.BlockSpec((8, _TT), lambda s: (0, s)),
            pl.BlockSpec((_TT, _LANES), lambda s: (s, 0)),
            pl.BlockSpec((_TT, _LANES), lambda s: (s, 0)),
            pl.BlockSpec((1, 8, _LANES), lambda s: (s, 0, 0)),
            pl.BlockSpec((8, _LANES), lambda s: (0, 0)),
        ],
        out_shape=[
            jax.ShapeDtypeStruct((n, d), F32),
            jax.ShapeDtypeStruct((8, n), jnp.int32),
            jax.ShapeDtypeStruct((n, _LANES), jnp.int32),
            jax.ShapeDtypeStruct((n, _LANES), F32),
            jax.ShapeDtypeStruct((nt, 8, _LANES), jnp.int32),
            jax.ShapeDtypeStruct((8, _LANES), F32),
        ],
        scratch_shapes=[pltpu.VMEM((8, _LANES), F32)],
        compiler_params=_cparams("arbitrary"),
        name="moe_router",
    )(xf, mod, g, wr, br)


def _block_sizes(max_groups):
    return tuple(1 << b for b in range(max_groups.bit_length() - 1, -1, -1))


def _for_each_block(groups, sizes, fn):
    for size in sizes:
        shift = size.bit_length()
        done = ((groups >> shift) << shift) * _RUN

        @pl.when((groups & size) != 0)
        def _():
            fn(done, size * _RUN)


def _dispatch_kernel(tab_ref, tail_ref, slot_ref, h_ref, xs_ref, xt_ref, sem):
    s = pl.program_id(0)
    slots = slot_ref[...]
    sio = lax.broadcasted_iota(jnp.int32, (_SLOTS, _TT), 0)
    perm = jnp.zeros((_SLOTS, _TT), F32)
    for kk in range(TOP_K):
        perm = perm + (sio == slots[kk:kk + 1]).astype(F32)
    xt_ref[...] = _dot(perm.astype(BF16), h_ref[...].astype(BF16))

    def run_blocks(e, act):
        first = tab_ref[0, 1, e]
        base = tab_ref[0, 2, e]

        def block(off, rows):
            act(pltpu.make_async_copy(
                xt_ref.at[pl.ds(pl.multiple_of(first + off, _RUN), rows)],
                xs_ref.at[pl.ds(pl.multiple_of(base + off, _RUN), rows)], sem))

        _for_each_block(tab_ref[0, 0, e], _block_sizes(_TT // _RUN), block)

    def issue(e, carry):
        run_blocks(e, lambda cp: cp.start())
        return carry

    def drain(e, carry):
        run_blocks(e, lambda cp: cp.wait())
        return carry

    lax.fori_loop(0, N_EXPERTS, issue, 0)
    lax.fori_loop(0, N_EXPERTS, drain, 0)

    @pl.when(s == pl.num_programs(0) - 1)
    def _():
        xt_ref[0:_TM, :] = jnp.zeros((_TM, xt_ref.shape[1]), F32)

        def tail_blocks(e, act):
            base = tail_ref[1, e]

            def block(off, rows):
                act(pltpu.make_async_copy(
                    xt_ref.at[pl.ds(0, rows)],
                    xs_ref.at[pl.ds(pl.multiple_of(base + off, _RUN), rows)], sem))

            _for_each_block(tail_ref[0, e], _block_sizes(_TM // _RUN - 1), block)

        def issue_tail(e, carry):
            tail_blocks(e, lambda cp: cp.start())
            return carry

        def drain_tail(e, carry):
            tail_blocks(e, lambda cp: cp.wait())
            return carry

        lax.fori_loop(0, N_EXPERTS, issue_tail, 0)
        lax.fori_loop(0, N_EXPERTS, drain_tail, 0)

        def free_tile(i):
            row0 = pl.multiple_of(tail_ref[2, 1] + i * _TM, _TM)
            return pltpu.make_async_copy(xt_ref.at[pl.ds(0, _TM)], xs_ref.at[pl.ds(row0, _TM)], sem)

        def issue_free(i, carry):
            free_tile(i).start()
            return carry

        def drain_free(i, carry):
            free_tile(i).wait()
            return carry

        lax.fori_loop(0, tail_ref[2, 0], issue_free, 0)
        lax.fori_loop(0, tail_ref[2, 0], drain_free, 0)


def _dispatch(tab, tail, slot, h, rows):
    n, d = h.shape
    nt = n // _TT
    return pl.pallas_call(
        _dispatch_kernel,
        grid=(nt,),
        in_specs=[
            pl.BlockSpec((1, 8, _LANES), lambda s: (s, 0, 0), memory_space=pltpu.SMEM),
            pl.BlockSpec((8, _LANES), lambda s: (0, 0), memory_space=pltpu.SMEM),
            pl.BlockSpec((8, _TT), lambda s: (0, s)),
            pl.BlockSpec((_TT, d), lambda s: (s, 0)),
        ],
        out_specs=pl.BlockSpec(memory_space=pl.ANY),
        out_shape=jax.ShapeDtypeStruct((rows, d), F32),
        scratch_shapes=[pltpu.VMEM((_SLOTS, d), F32), pltpu.SemaphoreType.DMA],
        compiler_params=_cparams("arbitrary"),
        name="moe_dispatch",
    )(tab, tail, slot, h)


def _expert_kernel(te_ref, nu_ref, xs_ref, wgu_ref, bgu_ref, wd_ref, bd_ref, ys_ref,
                   wgu_bf_ref, wd_bf_ref):
    t = pl.program_id(0)
    live = t < nu_ref[0]
    new_expert = jnp.logical_or(t == 0, te_ref[t] != te_ref[jnp.maximum(t - 1, 0)])

    @pl.when(jnp.logical_and(live, new_expert))
    def _():
        wgu_bf_ref[...] = wgu_ref[0, 0].astype(BF16)
        wd_bf_ref[...] = wd_ref[0, 0].astype(BF16)

    @pl.when(live)
    def _():
        x = xs_ref[...].astype(BF16)
        gu = _dot(x, wgu_bf_ref[...]) + bgu_ref[0, 0]
        gate = jnp.minimum(gu[:, 0:D_EXPERT], SWIGLU_LIMIT)
        up = jnp.clip(gu[:, D_EXPERT:], -SWIGLU_LIMIT, SWIGLU_LIMIT)
        act = (up + 1.0) * (gate * _sigmoid(SWIGLU_ALPHA * gate))
        ys_ref[...] = _dot(act.astype(BF16), wd_bf_ref[...]) + bd_ref[0, 0]

    @pl.when(jnp.logical_not(live))
    def _():
        ys_ref[...] = jnp.zeros_like(ys_ref)


def _experts(tile_expert, n_used, xs, layer, wgu, bgu, wd, bd):
    p, d = xs.shape
    nt = p // _TM
    depth, e, _, n2 = wgu.shape
    grid_spec = pltpu.PrefetchScalarGridSpec(
        num_scalar_prefetch=2,
        grid=(nt,),
        in_specs=[
            pl.BlockSpec((_TM, d), lambda t, te, nu: (jnp.minimum(t, nu[0] - 1), 0)),
            pl.BlockSpec((1, 1, d, n2), lambda t, te, nu: (layer, te[t], 0, 0)),
            pl.BlockSpec((1, 1, 1, n2), lambda t, te, nu: (layer, te[t], 0, 0)),
            pl.BlockSpec((1, 1, D_EXPERT, d), lambda t, te, nu: (layer, te[t], 0, 0)),
            pl.BlockSpec((1, 1, 1, d), lambda t, te, nu: (layer, te[t], 0, 0)),
        ],
        out_specs=pl.BlockSpec((_TM, d), lambda t, te, nu: (t, 0)),
        scratch_shapes=[pltpu.VMEM((d, n2), BF16), pltpu.VMEM((D_EXPERT, d), BF16)],
    )
    return pl.pallas_call(
        _expert_kernel,
        grid_spec=grid_spec,
        out_shape=jax.ShapeDtypeStruct((p, d), F32),
        compiler_params=_cparams("arbitrary"),
        name="moe_experts",
    )(tile_expert, n_used, xs, wgu, bgu.reshape(depth, e, 1, n2), wd, bd.reshape(depth, e, 1, d))


def _combine_kernel(tab_ref, slotc_ref, w_ref, ys_ref, x_ref, mod_ref, o_ref, yb_ref, sem):
    @pl.when(pl.program_id(0) == 0)
    def _():
        yb_ref[...] = jnp.zeros_like(yb_ref)

    def run_blocks(e, act):
        first = tab_ref[0, 1, e]
        base = tab_ref[0, 2, e]

        def block(off, rows):
            act(pltpu.make_async_copy(
                ys_ref.at[pl.ds(pl.multiple_of(base + off, _RUN), rows)],
                yb_ref.at[pl.ds(pl.multiple_of(first + off, _RUN), rows)], sem))

        _for_each_block(tab_ref[0, 0, e], _block_sizes(_TT // _RUN), block)

    def issue(e, carry):
        run_blocks(e, lambda cp: cp.start())
        return carry

    def drain(e, carry):
        run_blocks(e, lambda cp: cp.wait())
        return carry

    lax.fori_loop(0, N_EXPERTS, issue, 0)
    lax.fori_loop(0, N_EXPERTS, drain, 0)
    used = tab_ref[0, 1, N_EXPERTS - 1] + tab_ref[0, 0, N_EXPERTS - 1] * _RUN
    rio = lax.broadcasted_iota(jnp.int32, (_SLOTS, 1), 0)
    y = jnp.where(rio < used, yb_ref[...], 0.0).astype(BF16)
    lio = lax.broadcasted_iota(jnp.int32, (_TT, _SLOTS), 1)
    slotc = slotc_ref[...]
    w = w_ref[...]
    gates = jnp.zeros((_TT, _SLOTS), F32)
    for kk in range(TOP_K):
        gates = gates + jnp.where(lio == slotc[:, kk:kk + 1], w[:, kk:kk + 1], 0.0)
    f = _dot(gates.astype(BF16), y)
    o_ref[...] = x_ref[...] + mod_ref[0, 0][5:6] * f


def _combine(tab, slotc, w, ys, xf, mod, ntb, nct):
    n, d = xf.shape
    nt = n // _TT
    seg = lambda s: jnp.where(s % ntb >= nct, 1, 0)
    return pl.pallas_call(
        _combine_kernel,
        grid=(nt,),
        in_specs=[
            pl.BlockSpec((1, 8, _LANES), lambda s: (s, 0, 0), memory_space=pltpu.SMEM),
            pl.BlockSpec((_TT, _LANES), lambda s: (s, 0)),
            pl.BlockSpec((_TT, _LANES), lambda s: (s, 0)),
            pl.BlockSpec(memory_space=pl.ANY),
            pl.BlockSpec((_TT, d), lambda s: (s, 0)),
            pl.BlockSpec((1, 1, N_MOD, d), lambda s: (s // ntb, seg(s), 0, 0)),
        ],
        out_specs=pl.BlockSpec((_TT, d), lambda s: (s, 0)),
        out_shape=jax.ShapeDtypeStruct((n, d), F32),
        scratch_shapes=[pltpu.VMEM((_SLOTS, d), F32), pltpu.SemaphoreType.DMA],
        compiler_params=_cparams("arbitrary"),
        name="moe_combine",
    )(tab, slotc, w, ys, xf, mod)


def _moe(xa, mod, g2, wr, br, layer, wgu, bgu, wd, bd, nct):
    b, t, d = xa.shape
    n = b * t
    ntb = t // _TT
    xf = xa.reshape(n, d)
    nt = n // _TT
    h, slot, slotc, w, tab, tot = _router(xf, mod, g2, wr, br, ntb, nct)
    run_rows = tot[0].astype(jnp.int32)
    padded = (run_rows + _TM - 1) // _TM * _TM
    ends = jnp.cumsum(padded)
    starts = ends - padded
    row = lax.broadcasted_iota(jnp.int32, (8, _LANES), 0)
    tab = tab + jnp.where(row == 2, starts[None, :], 0)[None]
    max_rows = n * TOP_K + nt * N_EXPERTS * (_RUN - 1) + N_EXPERTS * (_TM - 1)
    n_tiles = max_rows // _TM
    lane = lax.broadcasted_iota(jnp.int32, (8, _LANES), 1)
    used_rows = ends[N_EXPERTS - 1]
    free = jnp.where(lane == 0, n_tiles - used_rows // _TM, jnp.where(lane == 1, used_rows, 0))
    tail = jnp.where(row == 0, ((padded - run_rows) // _RUN)[None, :],
                     jnp.where(row == 1, (starts + run_rows)[None, :],
                               jnp.where(row == 2, free, 0)))
    tile_row = jnp.arange(n_tiles, dtype=jnp.int32) * _TM
    tile_expert = jnp.minimum(jnp.sum(tile_row[:, None] >= ends[None, :N_EXPERTS], axis=1),
                              N_EXPERTS - 1).astype(jnp.int32)
    n_used = (ends[N_EXPERTS - 1:N_EXPERTS] // _TM).astype(jnp.int32)
    xs = _dispatch(tab, tail, slot, h, n_tiles * _TM)
    ys = _experts(tile_expert, n_used, xs, layer, wgu, bgu, wd, bd)
    return _combine(tab, slotc, w, ys, xf, mod, ntb, nct).reshape(b, t, d)


def _rope_tables(n_ctx, n_lat):
    rows = n_lat // GRID_W
    row_ids = jnp.repeat(jnp.arange(rows), GRID_W).astype(F32)
    col_ids = jnp.tile(jnp.arange(GRID_W), rows).astype(F32)
    axis_dim = HEAD_DIM // 2
    inv_freq = 1.0 / (ROPE_THETA ** (jnp.arange(0, axis_dim, 2, dtype=F32) / axis_dim))
    ang = jnp.concatenate([row_ids[:, None] * inv_freq, col_ids[:, None] * inv_freq], axis=-1)
    cos = jnp.concatenate([jnp.ones((n_ctx, axis_dim), F32), jnp.cos(ang)], axis=0)
    sin = jnp.concatenate([jnp.zeros((n_ctx, axis_dim), F32), jnp.sin(ang)], axis=0)
    return cos.T, sin.T


def kernel(x, c, ctx, c_ctx, ada_w, ada_b, norm1_g, norm2_g, attn_w_qkv, attn_q_gain, attn_k_gain, attn_w_o, gla_w_in, gla_w_gk2_f, gla_b_gk_f, gla_w_gk2_b, gla_b_gk_b, gla_o_gain, gla_w_o, moe_w_router, moe_b_router, moe_w_gu, moe_b_gu, moe_w_down, moe_b_down):
    b, n_lat, d = x.shape
    n_ctx = ctx.shape[1]
    assert d == D_MODEL and n_ctx % _TT == 0 and n_lat % _TT == 0 and n_lat % GRID_W == 0
    nct = n_ctx // _TT
    depth = ada_w.shape[0]

    r = -(-(b + 1) // 8) * 8
    cond = jnp.zeros((r, d), F32).at[:b].set(c).at[b].set(c_ctx)
    table = _ada_table(cond, ada_w, ada_b).reshape(depth, r, N_MOD, d)
    mods = jnp.stack([jnp.broadcast_to(table[:, b:b + 1], (depth, b, N_MOD, d)), table[:, :b]],
                     axis=2)

    cos_t, sin_t = _rope_tables(n_ctx, n_lat)
    perm = jnp.concatenate([jnp.arange(0, HEAD_DIM, 2), jnp.arange(1, HEAD_DIM, 2)])
    qk_cols = (jnp.arange(N_Q_HEADS + N_KV_HEADS)[:, None] * HEAD_DIM + perm[None, :]).reshape(-1)
    cols = jnp.concatenate([qk_cols, jnp.arange(Q_DIM + KV_DIM, QKV_DIM)])

    xa = jnp.concatenate([ctx, x], axis=1)
    for i in range(depth):
        mod = mods[i]
        j = i // N_MIXERS
        g1 = norm1_g[i].reshape(1, d)
        if i % N_MIXERS == 0:
            wt = attn_w_qkv[j][:, cols].T.astype(BF16)
            gq = attn_q_gain[j][perm].reshape(HEAD_DIM, 1)
            gk = attn_k_gain[j][perm].reshape(HEAD_DIM, 1)
            qt, k, vt = _qkv_project(xa, mod, g1, wt, gq, gk, cos_t, sin_t, nct)
            xa = _attention(qt, k, vt, attn_w_o[j].astype(BF16), xa, mod, nct)
        else:
            n_main = 2 * GLA_DK + 2 * GLA_DV
            wm = gla_w_in[j][:, :n_main].astype(BF16)
            wl = jnp.zeros((d, _LANES), F32).at[:, :2 * GLA_GATE_RANK].set(
                gla_w_in[j][:, n_main:]).astype(BF16)
            w2 = jnp.zeros((_LANES, 2 * GLA_DK), F32)
            w2 = w2.at[:GLA_GATE_RANK, :GLA_DK].set(gla_w_gk2_f[j])
            w2 = w2.at[GLA_GATE_RANK:2 * GLA_GATE_RANK, GLA_DK:].set(gla_w_gk2_b[j])
            b2 = jnp.concatenate([gla_b_gk_f[j], gla_b_gk_b[j]]).reshape(1, 2 * GLA_DK)
            q, k, v, rr, gf, gb = _gla_project(xa, mod, g1, wm, wl, w2, b2, nct)
            o_f, o_b = _gla_scan(q, k, v, gf, gb, nct)
            xa = _gla_readout(o_f, o_b, rr, gla_o_gain[j].reshape(1, GLA_DV_HEAD),
                              gla_w_o[j].astype(BF16), xa, mod, nct)
        wr = jnp.zeros((d, _LANES), F32).at[:, :N_EXPERTS].set(moe_w_router[i])
        br = jnp.full((1, _LANES), _NEG, F32).at[0, :N_EXPERTS].set(moe_b_router[i])
        xa = _moe(xa, mod, norm2_g[i].reshape(1, d), wr, br,
                  i, moe_w_gu, moe_b_gu, moe_w_down, moe_b_down, nct)
    return xa[:, n_ctx:]
```

```python
import functools

import jax
import jax.numpy as jnp
from jax import lax
from jax.experimental import pallas as pl
from jax.experimental.pallas import tpu as pltpu

F32 = jnp.float32
BF16 = jnp.bfloat16

D_MODEL = 1024
DEPTH = 4
GRID_W = 64
N_MIXERS = 2
N_MOD = 6
RMS_EPS = 1e-6
HEAD_DIM = 64
N_Q_HEADS = D_MODEL // HEAD_DIM
N_KV_HEADS = N_Q_HEADS // 4
GQA_GROUP = N_Q_HEADS // N_KV_HEADS
Q_DIM = N_Q_HEADS * HEAD_DIM
KV_DIM = N_KV_HEADS * HEAD_DIM
QKV_DIM = Q_DIM + 2 * KV_DIM
V_ROWS = HEAD_DIM + 16
LOG2E = 1.4426950408889634
ROPE_THETA = 10000.0
GLA_HEADS = 4
GLA_DK = D_MODEL // 2
GLA_DV = D_MODEL
GLA_DK_HEAD = GLA_DK // GLA_HEADS
GLA_DV_HEAD = GLA_DV // GLA_HEADS
GLA_GATE_RANK = 16
GLA_GATE_NORM = 16.0
N_EXPERTS = 32
TOP_K = 4
D_EXPERT = D_MODEL
SWIGLU_LIMIT = 7.0
SWIGLU_ALPHA = 1.702

_LANES = 128
_TT = 256
_KC = 256
_HPI = 2
_GLA_SAFE_LOG_DECAY = 60.0
_TM = 256
_RUN = 8
_SLOTS = _TT * TOP_K + N_EXPERTS * _RUN
_ADA_TN = 1536
_VMEM_LIMIT = 56 * 1024 * 1024
_NEG = -1e30


def _cparams(*sem):
    return pltpu.CompilerParams(dimension_semantics=sem, vmem_limit_bytes=_VMEM_LIMIT)


def _dot(a, b):
    return jnp.dot(a, b, preferred_element_type=F32)


def _dot_nt(a, b):
    return lax.dot_general(a, b, (((1,), (1,)), ((), ())), preferred_element_type=F32)


def _split_bf16(a):
    hi = a.astype(BF16)
    lo = (a - hi.astype(F32)).astype(BF16)
    return hi, lo


def _dot3(a, w):
    ah, al = _split_bf16(a)
    wh, wl = _split_bf16(w)
    return _dot(ah, wh) + _dot(ah, wl) + _dot(al, wh)


def _modulate(x, g, shift, scale):
    ms = jnp.mean(x * x, axis=-1, keepdims=True)
    y = x * lax.rsqrt(ms + RMS_EPS) * g
    return y * (1.0 + scale) + shift


def _sigmoid(x):
    return 1.0 / (1.0 + jnp.exp(-x))


def _ada_kernel(cond_ref, w_ref, b_ref, o_ref):
    c = cond_ref[...]
    o_ref[0] = _dot3(c * _sigmoid(c), w_ref[0]) + b_ref[0]


def _ada_table(cond, ada_w, ada_b):
    depth, d, n = ada_w.shape
    r = cond.shape[0]
    return pl.pallas_call(
        _ada_kernel,
        grid=(depth, n // _ADA_TN),
        in_specs=[
            pl.BlockSpec((r, d), lambda i, j: (0, 0)),
            pl.BlockSpec((1, d, _ADA_TN), lambda i, j: (i, 0, j)),
            pl.BlockSpec((1, 1, _ADA_TN), lambda i, j: (i, 0, j)),
        ],
        out_specs=pl.BlockSpec((1, r, _ADA_TN), lambda i, j: (i, 0, j)),
        out_shape=jax.ShapeDtypeStruct((depth, r, n), F32),
        compiler_params=_cparams("parallel", "parallel"),
        name="ada_table",
    )(cond, ada_w, ada_b.reshape(depth, 1, n))


def _qkv_kernel(x_ref, mod_ref, g_ref, wt_ref, gq_ref, gk_ref, cos_ref, sin_ref,
                qt_ref, k_ref, vt_ref):
    mod = mod_ref[0, 0]
    h = _modulate(x_ref[0], g_ref[...], mod[0:1], mod[1:2])
    ht = h.T.astype(BF16)
    qkvt = _dot(wt_ref[...], ht)
    c = cos_ref[...]
    s = sin_ref[...]
    half = HEAD_DIM // 2

    def norm_rope(blk, gain, scale):
        ms = jnp.mean(blk * blk, axis=0, keepdims=True)
        n = blk * lax.rsqrt(ms + RMS_EPS) * gain
        x1 = n[0:half]
        x2 = n[half:HEAD_DIM]
        return jnp.concatenate([x1 * c - x2 * s, x1 * s + x2 * c], axis=0) * scale

    gq = gq_ref[...]
    gk = gk_ref[...]
    for hh in range(N_Q_HEADS):
        r0 = hh * HEAD_DIM
        qt_ref[0, r0:r0 + HEAD_DIM, :] = norm_rope(
            qkvt[r0:r0 + HEAD_DIM], gq, LOG2E * HEAD_DIM ** -0.5).astype(BF16)
    kt = jnp.concatenate(
        [norm_rope(qkvt[Q_DIM + j * HEAD_DIM:Q_DIM + (j + 1) * HEAD_DIM], gk, 1.0)
         for j in range(N_KV_HEADS)], axis=0)
    k_ref[0] = kt.T.astype(BF16)
    ones = jnp.ones((V_ROWS - HEAD_DIM, vt_ref.shape[2]), BF16)
    for j in range(N_KV_HEADS):
        v0 = Q_DIM + KV_DIM + j * HEAD_DIM
        vt_ref[0, j * V_ROWS:j * V_ROWS + HEAD_DIM, :] = qkvt[v0:v0 + HEAD_DIM].astype(BF16)
        vt_ref[0, j * V_ROWS + HEAD_DIM:(j + 1) * V_ROWS, :] = ones


def _qkv_project(xa, mod, g, wt, gq, gk, cos_t, sin_t, nct):
    b, t, d = xa.shape
    nt = t // _TT
    seg = lambda j: jnp.where(j >= nct, 1, 0)
    return pl.pallas_call(
        _qkv_kernel,
        grid=(b, nt),
        in_specs=[
            pl.BlockSpec((1, _TT, d), lambda i, j: (i, j, 0)),
            pl.BlockSpec((1, 1, N_MOD, d), lambda i, j: (i, seg(j), 0, 0)),
            pl.BlockSpec((1, d), lambda i, j: (0, 0)),
            pl.BlockSpec((QKV_DIM, d), lambda i, j: (0, 0)),
            pl.BlockSpec((HEAD_DIM, 1), lambda i, j: (0, 0)),
            pl.BlockSpec((HEAD_DIM, 1), lambda i, j: (0, 0)),
            pl.BlockSpec((HEAD_DIM // 2, _TT), lambda i, j: (0, j)),
            pl.BlockSpec((HEAD_DIM // 2, _TT), lambda i, j: (0, j)),
        ],
        out_specs=[
            pl.BlockSpec((1, Q_DIM, _TT), lambda i, j: (i, 0, j)),
            pl.BlockSpec((1, _TT, KV_DIM), lambda i, j: (i, j, 0)),
            pl.BlockSpec((1, N_KV_HEADS * V_ROWS, _TT), lambda i, j: (i, 0, j)),
        ],
        out_shape=[
            jax.ShapeDtypeStruct((b, Q_DIM, t), BF16),
            jax.ShapeDtypeStruct((b, t, KV_DIM), BF16),
            jax.ShapeDtypeStruct((b, N_KV_HEADS * V_ROWS, t), BF16),
        ],
        compiler_params=_cparams("parallel", "parallel"),
        name="attn_qkv",
    )(xa, mod, g, wt, gq, gk, cos_t, sin_t)


def _attn_kernel(qt_ref, k_ref, vt_ref, wo_ref, x_ref, mod_ref, o_ref, qe_ref, ot_ref,
                 *, nct, n_ctx):
    j = pl.program_id(1)
    t_all = k_ref.shape[1]

    def heads(nk):
        def head_pair(hp, carry):
            kvh = hp // (GQA_GROUP // _HPI)
            k0 = pl.multiple_of(kvh * HEAD_DIM, HEAD_DIM)
            v0 = pl.multiple_of(kvh * V_ROWS, 16)
            for u in range(_HPI):
                q0 = pl.multiple_of((_HPI * hp + u) * HEAD_DIM, HEAD_DIM)
                qe_ref[u] = jnp.zeros(qe_ref.shape[1:], BF16)
                qe_ref[u, pl.ds(k0, HEAD_DIM), :] = qt_ref[0, pl.ds(q0, HEAD_DIM), :]
            bounds = list(range(0, nk, _KC)) + [nk]
            chunks = list(zip(bounds[:-1], bounds[1:]))

            st = [_dot(k_ref[0, 0:nk, :], qe_ref[u]) for u in range(_HPI)]
            acc = [jnp.zeros((V_ROWS, _TT), F32) for _ in range(_HPI)]
            m_run = [jnp.full((1, _TT), _NEG, F32) for _ in range(_HPI)]
            for lo, hi in chunks:
                for u in range(_HPI):
                    s_cur = st[u][lo:hi]
                    m_new = jnp.maximum(m_run[u], jnp.max(s_cur, axis=0, keepdims=True))
                    p = jnp.exp2(s_cur - m_new).astype(BF16)
                    pv = _dot(vt_ref[0, pl.ds(v0, V_ROWS), lo:hi], p)
                    acc[u] = acc[u] * jnp.exp2(m_run[u] - m_new) + pv
                    m_run[u] = m_new
            for u in range(_HPI):
                q0 = pl.multiple_of((_HPI * hp + u) * HEAD_DIM, HEAD_DIM)
                ot_ref[pl.ds(q0, HEAD_DIM), :] = acc[u][0:HEAD_DIM] / acc[u][HEAD_DIM:HEAD_DIM + 1]
            return carry

        lax.fori_loop(0, N_Q_HEADS // _HPI, head_pair, 0)

    @pl.when(j < nct)
    def _():
        heads(n_ctx)

    @pl.when(j >= nct)
    def _():
        heads(t_all)

    att = ot_ref[...].T.astype(BF16)
    out = _dot(att, wo_ref[...])
    o_ref[0] = x_ref[0] + mod_ref[0, 0][2:3] * out


def _attention(qt, k, vt, wo, xa, mod, nct):
    b, t, d = xa.shape
    nt = t // _TT
    seg = lambda j: jnp.where(j >= nct, 1, 0)
    kern = functools.partial(_attn_kernel, nct=nct, n_ctx=nct * _TT)
    return pl.pallas_call(
        kern,
        grid=(b, nt),
        in_specs=[
            pl.BlockSpec((1, Q_DIM, _TT), lambda i, j: (i, 0, j)),
            pl.BlockSpec((1, t, KV_DIM), lambda i, j: (i, 0, 0)),
            pl.BlockSpec((1, N_KV_HEADS * V_ROWS, t), lambda i, j: (i, 0, 0)),
            pl.BlockSpec((Q_DIM, d), lambda i, j: (0, 0)),
            pl.BlockSpec((1, _TT, d), lambda i, j: (i, j, 0)),
            pl.BlockSpec((1, 1, N_MOD, d), lambda i, j: (i, seg(j), 0, 0)),
        ],
        out_specs=pl.BlockSpec((1, _TT, d), lambda i, j: (i, j, 0)),
        out_shape=jax.ShapeDtypeStruct((b, t, d), F32),
        scratch_shapes=[pltpu.VMEM((_HPI, KV_DIM, _TT), BF16), pltpu.VMEM((Q_DIM, _TT), F32)],
        compiler_params=_cparams("parallel", "arbitrary"),
        name="attn_core",
    )(qt, k, vt, wo, xa, mod)


def _gla_in_kernel(x_ref, mod_ref, g_ref, wm_ref, wl_ref, w2_ref, b2_ref,
                   q_ref, k_ref, v_ref, r_ref, gf_ref, gb_ref):
    mod = mod_ref[0, 0]
    h = _modulate(x_ref[0], g_ref[...], mod[0:1], mod[1:2]).astype(BF16)
    main = _dot(h, wm_ref[...])
    q_ref[0] = (main[:, 0:GLA_DK] * GLA_DK_HEAD ** -0.5).astype(BF16)
    k_ref[0] = main[:, GLA_DK:2 * GLA_DK].astype(BF16)
    v_ref[0] = main[:, 2 * GLA_DK:2 * GLA_DK + GLA_DV].astype(BF16)
    r_ref[0] = main[:, 2 * GLA_DK + GLA_DV:].astype(BF16)
    low = _dot(h, wl_ref[...])
    pre = _dot3(low, w2_ref[...]) + b2_ref[...]
    logsig = jnp.minimum(pre, 0.0) - jnp.log(1.0 + jnp.exp(-jnp.abs(pre)))
    gate = logsig / GLA_GATE_NORM
    gf_ref[0] = gate[:, 0:GLA_DK]
    gb_ref[0] = gate[:, GLA_DK:]


def _gla_project(xa, mod, g, wm, wl, w2, b2, nct):
    b, t, d = xa.shape
    nt = t // _TT
    seg = lambda j: jnp.where(j >= nct, 1, 0)
    tile = lambda n: pl.BlockSpec((1, _TT, n), lambda i, j: (i, j, 0))
    full = lambda a: pl.BlockSpec(a.shape, lambda i, j: (0,) * a.ndim)
    return pl.pallas_call(
        _gla_in_kernel,
        grid=(b, nt),
        in_specs=[
            tile(d),
            pl.BlockSpec((1, 1, N_MOD, d), lambda i, j: (i, seg(j), 0, 0)),
            full(g), full(wm), full(wl), full(w2), full(b2),
        ],
        out_specs=[tile(GLA_DK), tile(GLA_DK), tile(GLA_DV), tile(GLA_DV),
                   tile(GLA_DK), tile(GLA_DK)],
        out_shape=[
            jax.ShapeDtypeStruct((b, t, GLA_DK), BF16),
            jax.ShapeDtypeStruct((b, t, GLA_DK), BF16),
            jax.ShapeDtypeStruct((b, t, GLA_DV), BF16),
            jax.ShapeDtypeStruct((b, t, GLA_DV), BF16),
            jax.ShapeDtypeStruct((b, t, GLA_DK), F32),
            jax.ShapeDtypeStruct((b, t, GLA_DK), F32),
        ],
        compiler_params=_cparams("parallel", "parallel"),
        name="gla_in",
    )(xa, mod, g, wm, wl, w2, b2)


def _gla_scan_kernel(qf_ref, kf_ref, vf_ref, gf_ref, qb_ref, kb_ref, vb_ref, gb_ref,
                     tril_ref, triu_ref, of_ref, ob_ref, st_ref, a_ref, kx_ref, bx_ref):
    @pl.when(pl.program_id(1) == 0)
    def _():
        st_ref[...] = jnp.zeros_like(st_ref)

    chains = []
    for h in range(GLA_HEADS):
        kcols = slice(h * GLA_DK_HEAD, (h + 1) * GLA_DK_HEAD)
        vcols = slice(h * GLA_DV_HEAD, (h + 1) * GLA_DV_HEAD)
        chains.append((qf_ref, kf_ref, vf_ref, gf_ref, tril_ref, of_ref, kcols, vcols, False))
        chains.append((qb_ref, kb_ref, vb_ref, gb_ref, triu_ref, ob_ref, kcols, vcols, True))

    work = []
    for ci, (q_ref, k_ref, v_ref, g_ref, tri_ref, o_ref, kcols, vcols, reverse) in enumerate(chains):
        tri = tri_ref[...]
        gh, gl = _split_bf16(g_ref[0, :, kcols])
        bcum = _dot(tri, gh) + _dot(tri, gl)
        b_end = bcum[0:1] if reverse else bcum[_TT - 1:_TT]
        q = q_ref[0, :, kcols].astype(F32)
        k = k_ref[0, :, kcols].astype(F32)
        qd = (q * jnp.exp(bcum)).astype(BF16)
        kd = (k * jnp.exp(b_end - bcum)).astype(BF16)
        a_ref[ci] = _dot_nt(qd, (k * jnp.exp(-bcum)).astype(BF16))
        work.append((q, k, bcum, b_end, qd, kd))

    for ci, (q, k, bcum, b_end, qd, kd) in enumerate(work):
        @pl.when(jnp.max(-b_end) > _GLA_SAFE_LOG_DECAY)
        def _():
            kx_ref[...] = k
            bx_ref[...] = bcum
            lane = lax.broadcasted_iota(jnp.int32, (_TT, _TT), 1)

            def column(j, carry):
                decay = jnp.exp(jnp.minimum(bcum - bx_ref[pl.ds(j, 1), :], 0.0))
                col = jnp.sum(q * decay * kx_ref[pl.ds(j, 1), :], axis=1, keepdims=True)
                a_ref[ci] = jnp.where(lane == j, col, a_ref[ci])
                return carry

            lax.fori_loop(0, _TT, column, 0)

    for ci, (q_ref, k_ref, v_ref, g_ref, tri_ref, o_ref, kcols, vcols, reverse) in enumerate(chains):
        q, k, bcum, b_end, qd, kd = work[ci]
        v = v_ref[0, :, vcols]
        a = jnp.where(tri_ref[...] > 0, a_ref[ci], 0.0).astype(BF16)
        st = st_ref[ci]
        o_ref[0, :, vcols] = _dot_nt(qd, st.astype(BF16)) + _dot(a, v)
        vt = v.astype(F32).T.astype(BF16)
        st_ref[ci] = st * jnp.exp(b_end) + _dot(vt, kd)


def _gla_scan(q, k, v, gf, gb, nct):
    b, t, _ = q.shape
    nb = t // _TT

    def rev(s):
        return jnp.where(s < nct, nct - 1 - s, nb - 1 - (s - nct))

    kspec = pl.BlockSpec((1, _TT, GLA_DK), lambda i, s: (i, s, 0))
    vspec = pl.BlockSpec((1, _TT, GLA_DV), lambda i, s: (i, s, 0))
    kspec_r = pl.BlockSpec((1, _TT, GLA_DK), lambda i, s: (i, rev(s), 0))
    vspec_r = pl.BlockSpec((1, _TT, GLA_DV), lambda i, s: (i, rev(s), 0))
    tspec = pl.BlockSpec((_TT, _TT), lambda i, s: (0, 0))
    row = lax.broadcasted_iota(jnp.int32, (_TT, _TT), 0)
    col = lax.broadcasted_iota(jnp.int32, (_TT, _TT), 1)
    n_chains = 2 * GLA_HEADS
    return pl.pallas_call(
        _gla_scan_kernel,
        grid=(b, nb),
        in_specs=[kspec, kspec, vspec, kspec, kspec_r, kspec_r, vspec_r, kspec_r, tspec, tspec],
        out_specs=[vspec, vspec_r],
        out_shape=[jax.ShapeDtypeStruct((b, t, GLA_DV), F32)] * 2,
        scratch_shapes=[pltpu.VMEM((n_chains, GLA_DV_HEAD, GLA_DK_HEAD), F32),
                        pltpu.VMEM((n_chains, _TT, _TT), F32),
                        pltpu.VMEM((_TT, GLA_DK_HEAD), F32),
                        pltpu.VMEM((_TT, GLA_DK_HEAD), F32)],
        compiler_params=_cparams("parallel", "arbitrary"),
        name="gla_scan",
    )(q, k, v, gf, q, k, v, gb, (col <= row).astype(BF16), (col >= row).astype(BF16))


def _gla_out_kernel(of_ref, ob_ref, r_ref, og_ref, wo_ref, x_ref, mod_ref, o_ref):
    o = of_ref[0] + ob_ref[0]
    og = og_ref[...]
    parts = []
    for h in range(GLA_HEADS):
        oh = o[:, h * GLA_DV_HEAD:(h + 1) * GLA_DV_HEAD]
        ms = jnp.mean(oh * oh, axis=-1, keepdims=True)
        parts.append(oh * lax.rsqrt(ms + RMS_EPS) * og)
    r = r_ref[0].astype(F32)
    y = jnp.concatenate(parts, axis=1) * (r * _sigmoid(r))
    out = _dot(y.astype(BF16), wo_ref[...])
    o_ref[0] = x_ref[0] + mod_ref[0, 0][2:3] * out


def _gla_readout(o_f, o_b, r, og, wo, xa, mod, nct):
    b, t, d = xa.shape
    nt = t // _TT
    seg = lambda j: jnp.where(j >= nct, 1, 0)
    tile = lambda n: pl.BlockSpec((1, _TT, n), lambda i, j: (i, j, 0))
    return pl.pallas_call(
        _gla_out_kernel,
        grid=(b, nt),
        in_specs=[
            tile(GLA_DV), tile(GLA_DV), tile(GLA_DV),
            pl.BlockSpec((1, GLA_DV_HEAD), lambda i, j: (0, 0)),
            pl.BlockSpec((GLA_DV, d), lambda i, j: (0, 0)),
            tile(d),
            pl.BlockSpec((1, 1, N_MOD, d), lambda i, j: (i, seg(j), 0, 0)),
        ],
        out_specs=tile(d),
        out_shape=jax.ShapeDtypeStruct((b, t, d), F32),
        compiler_params=_cparams("parallel", "parallel"),
        name="gla_out",
    )(o_f, o_b, r, og, wo, xa, mod)


def _pad_run(count):
    return jnp.floor((count + (_RUN - 1)) * (1.0 / _RUN)) * _RUN


def _router_kernel(x_ref, mod_ref, g_ref, wr_ref, br_ref, h_ref, slot_ref, slotc_ref, w_ref,
                   tab_ref, tot_ref, carry_ref):
    i = pl.program_id(0)

    @pl.when(i == 0)
    def _():
        carry_ref[...] = jnp.zeros_like(carry_ref)

    mod = mod_ref[0, 0]
    h = _modulate(x_ref[...], g_ref[...], mod[3:4], mod[4:5])
    h_ref[...] = h
    logits = _dot3(h, wr_ref[...]) + br_ref[...]
    lt = logits.T[0:N_EXPERTS]
    eio = lax.broadcasted_iota(jnp.int32, lt.shape, 0)
    vals, hots = [], []
    for kk in range(TOP_K):
        m = jnp.max(lt, axis=0, keepdims=True)
        ik = jnp.min(jnp.where(lt == m, eio, N_EXPERTS), axis=0, keepdims=True)
        hot = eio == ik
        lt = jnp.where(hot, -jnp.inf, lt)
        vals.append(m)
        hots.append(hot)
    es = [jnp.exp(v - vals[0]) for v in vals]
    denom = es[0] + es[1] + es[2] + es[3]
    member = jnp.zeros(lt.shape, F32)
    for hot in hots:
        member = member + hot.astype(F32)
    member_bf = member.astype(BF16)

    r_t = lax.broadcasted_iota(jnp.int32, (_TT, _TT), 0)
    c_t = lax.broadcasted_iota(jnp.int32, (_TT, _TT), 1)
    before = (r_t < c_t).astype(BF16)
    in_run = _dot(member_bf, before)
    run_col = _pad_run(jnp.sum(member, axis=1, keepdims=True))
    r_e = lax.broadcasted_iota(jnp.int32, (N_EXPERTS, N_EXPERTS), 0)
    c_e = lax.broadcasted_iota(jnp.int32, (N_EXPERTS, N_EXPERTS), 1)
    first_col = _dot((c_e < r_e).astype(BF16),
                     jnp.broadcast_to(run_col, (N_EXPERTS, _LANES)).astype(BF16))[:, 0:1]
    slot = first_col + in_run
    rsub = lax.broadcasted_iota(jnp.int32, (_LANES, _TT), 0)
    slot_rows = jnp.zeros((_LANES, _TT), F32)
    w_rows = jnp.zeros((_LANES, _TT), F32)
    for kk in range(TOP_K):
        sk = jnp.sum(jnp.where(hots[kk], slot, 0.0), axis=0, keepdims=True)
        slot_rows = jnp.where(rsub == kk, sk, slot_rows)
        w_rows = jnp.where(rsub == kk, es[kk] / denom, w_rows)
    slot_ref[...] = slot_rows[0:8].astype(jnp.int32)
    slotc_ref[...] = slot_rows.T.astype(jnp.int32)
    w_ref[...] = w_rows.T

    member_pad = jnp.concatenate(
        [member_bf, jnp.zeros((_LANES - N_EXPERTS, _TT), BF16)], axis=0)
    run_row = _pad_run(_dot_nt(jnp.ones((8, _TT), BF16), member_pad))
    r_l = lax.broadcasted_iota(jnp.int32, (_LANES, _LANES), 0)
    c_l = lax.broadcasted_iota(jnp.int32, (_LANES, _LANES), 1)
    first_row = _dot(run_row.astype(BF16), (r_l < c_l).astype(BF16))
    earlier = carry_ref[...]
    tsub = lax.broadcasted_iota(jnp.int32, (8, _LANES), 0)
    tab = jnp.where(tsub == 0, run_row * (1.0 / _RUN),
                    jnp.where(tsub == 1, first_row, jnp.where(tsub == 2, earlier, 0.0)))
    tab_ref[0] = tab.astype(jnp.int32)
    carry_ref[...] = earlier + run_row
    tot_ref[...] = earlier + run_row


def _router(xf, mod, g, wr, br, ntb, nct):
    n, d = xf.shape
    nt = n // _TT
    seg = lambda s: jnp.where(s % ntb >= nct, 1, 0)
    return pl.pallas_call(
        _router_kernel,
        grid=(nt,),
        in_specs=[
            pl.BlockSpec((_TT, d), lambda s: (s, 0)),
            pl.BlockSpec((1, 1, N_MOD, d), lambda s: (s // ntb, seg(s), 0, 0)),
            pl.BlockSpec((1, d), lambda s: (0, 0)),
            pl.BlockSpec((d, _LANES), lambda s: (0, 0)),
            pl.BlockSpec((1, _LANES), lambda s: (0, 0)),
        ],
        out_specs=[
            pl.BlockSpec((_TT, d), lambda s: (s, 0)),
            pl.BlockSpec((8, _TT), lambda s: (0, s)),
            pl.BlockSpec((_TT, _LANES), lambda s: (s, 0)),
            pl.BlockSpec((_TT, _LANES), lambda s: (s, 0)),
            pl.BlockSpec((1, 8, _LANES), lambda s: (s, 0, 0)),
            pl.BlockSpec((8, _LANES), lambda s: (0, 0)),
        ],
        out_shape=[
            jax.ShapeDtypeStruct((n, d), F32),
            jax.ShapeDtypeStruct((8, n), jnp.int32),
            jax.ShapeDtypeStruct((n, _LANES), jnp.int32),
            jax.ShapeDtypeStruct((n, _LANES), F32),
            jax.ShapeDtypeStruct((nt, 8, _LANES), jnp.int32),
            jax.ShapeDtypeStruct((8, _LANES), F32),
        ],
        scratch_shapes=[pltpu.VMEM((8, _LANES), F32)],
        compiler_params=_cparams("arbitrary"),
        name="moe_router",
    )(xf, mod, g, wr, br)


def _block_sizes(max_groups):
    return tuple(1 << b for b in range(max_groups.bit_length() - 1, -1, -1))


def _for_each_block(groups, sizes, fn):
    for size in sizes:
        shift = size.bit_length()
        done = ((groups >> shift) << shift) * _RUN

        @pl.when((groups & size) != 0)
        def _():
            fn(done, size * _RUN)


def _dispatch_kernel(tab_ref, tail_ref, slot_ref, h_ref, xs_ref, xt_ref, sem):
    s = pl.program_id(0)
    slots = slot_ref[...]
    sio = lax.broadcasted_iota(jnp.int32, (_SLOTS, _TT), 0)
    perm = jnp.zeros((_SLOTS, _TT), F32)
    for kk in range(TOP_K):
        perm = perm + (sio == slots[kk:kk + 1]).astype(F32)
    xt_ref[...] = _dot(perm.astype(BF16), h_ref[...].astype(BF16))

    def run_blocks(e, act):
        first = tab_ref[0, 1, e]
        base = tab_ref[0, 2, e]

        def block(off, rows):
            act(pltpu.make_async_copy(
                xt_ref.at[pl.ds(pl.multiple_of(first + off, _RUN), rows)],
                xs_ref.at[pl.ds(pl.multiple_of(base + off, _RUN), rows)], sem))

        _for_each_block(tab_ref[0, 0, e], _block_sizes(_TT // _RUN), block)

    def issue(e, carry):
        run_blocks(e, lambda cp: cp.start())
        return carry

    lax.fori_loop(0, N_EXPERTS, issue, 0)
    used = tab_ref[0, 1, N_EXPERTS - 1] // _RUN + tab_ref[0, 0, N_EXPERTS - 1]
    _for_each_block(used, _block_sizes(_SLOTS // _RUN), lambda off, rows: pltpu.make_async_copy(
        xt_ref.at[pl.ds(0, rows)], xs_ref.at[pl.ds(0, rows)], sem).wait())

    @pl.when(s == pl.num_programs(0) - 1)
    def _():
        xt_ref[0:_TM, :] = jnp.zeros((_TM, xt_ref.shape[1]), F32)

        def tail_blocks(e, act):
            base = tail_ref[1, e]

            def block(off, rows):
                act(pltpu.make_async_copy(
                    xt_ref.at[pl.ds(0, rows)],
                    xs_ref.at[pl.ds(pl.multiple_of(base + off, _RUN), rows)], sem))

            _for_each_block(tail_ref[0, e], _block_sizes(_TM // _RUN - 1), block)

        def issue_tail(e, carry):
            tail_blocks(e, lambda cp: cp.start())
            return carry

        def drain_tail(e, carry):
            tail_blocks(e, lambda cp: cp.wait())
            return carry

        lax.fori_loop(0, N_EXPERTS, issue_tail, 0)
        lax.fori_loop(0, N_EXPERTS, drain_tail, 0)

        def free_tile(i):
            row0 = pl.multiple_of(tail_ref[2, 1] + i * _TM, _TM)
            return pltpu.make_async_copy(xt_ref.at[pl.ds(0, _TM)], xs_ref.at[pl.ds(row0, _TM)], sem)

        def issue_free(i, carry):
            free_tile(i).start()
            return carry

        def drain_free(i, carry):
            free_tile(i).wait()
            return carry

        lax.fori_loop(0, tail_ref[2, 0], issue_free, 0)
        lax.fori_loop(0, tail_ref[2, 0], drain_free, 0)


def _dispatch(tab, tail, slot, h, rows):
    n, d = h.shape
    nt = n // _TT
    return pl.pallas_call(
        _dispatch_kernel,
        grid=(nt,),
        in_specs=[
            pl.BlockSpec((1, 8, _LANES), lambda s: (s, 0, 0), memory_space=pltpu.SMEM),
            pl.BlockSpec((8, _LANES), lambda s: (0, 0), memory_space=pltpu.SMEM),
            pl.BlockSpec((8, _TT), lambda s: (0, s)),
            pl.BlockSpec((_TT, d), lambda s: (s, 0)),
        ],
        out_specs=pl.BlockSpec(memory_space=pl.ANY),
        out_shape=jax.ShapeDtypeStruct((rows, d), F32),
        scratch_shapes=[pltpu.VMEM((_SLOTS, d), F32), pltpu.SemaphoreType.DMA],
        compiler_params=_cparams("arbitrary"),
        name="moe_dispatch",
    )(tab, tail, slot, h)


def _expert_kernel(te_ref, nu_ref, xs_ref, wgu_ref, bgu_ref, wd_ref, bd_ref, ys_ref,
                   wgu_bf_ref, wd_bf_ref):
    t = pl.program_id(0)
    live = t < nu_ref[0]
    new_expert = jnp.logical_or(t == 0, te_ref[t] != te_ref[jnp.maximum(t - 1, 0)])

    @pl.when(jnp.logical_and(live, new_expert))
    def _():
        wgu_bf_ref[...] = wgu_ref[0, 0].astype(BF16)
        wd_bf_ref[...] = wd_ref[0, 0].astype(BF16)

    @pl.when(live)
    def _():
        x = xs_ref[...].astype(BF16)
        gu = _dot(x, wgu_bf_ref[...]) + bgu_ref[0, 0]
        gate = jnp.minimum(gu[:, 0:D_EXPERT], SWIGLU_LIMIT)
        up = jnp.clip(gu[:, D_EXPERT:], -SWIGLU_LIMIT, SWIGLU_LIMIT)
        act = (up + 1.0) * (gate * _sigmoid(SWIGLU_ALPHA * gate))
        ys_ref[...] = _dot(act.astype(BF16), wd_bf_ref[...]) + bd_ref[0, 0]

    @pl.when(jnp.logical_not(live))
    def _():
        ys_ref[...] = jnp.zeros_like(ys_ref)


def _experts(tile_expert, n_used, xs, layer, wgu, bgu, wd, bd):
    p, d = xs.shape
    nt = p // _TM
    depth, e, _, n2 = wgu.shape
    grid_spec = pltpu.PrefetchScalarGridSpec(
        num_scalar_prefetch=2,
        grid=(nt,),
        in_specs=[
            pl.BlockSpec((_TM, d), lambda t, te, nu: (jnp.minimum(t, nu[0] - 1), 0)),
            pl.BlockSpec((1, 1, d, n2), lambda t, te, nu: (layer, te[t], 0, 0)),
            pl.BlockSpec((1, 1, 1, n2), lambda t, te, nu: (layer, te[t], 0, 0)),
            pl.BlockSpec((1, 1, D_EXPERT, d), lambda t, te, nu: (layer, te[t], 0, 0)),
            pl.BlockSpec((1, 1, 1, d), lambda t, te, nu: (layer, te[t], 0, 0)),
        ],
        out_specs=pl.BlockSpec((_TM, d), lambda t, te, nu: (t, 0)),
        scratch_shapes=[pltpu.VMEM((d, n2), BF16), pltpu.VMEM((D_EXPERT, d), BF16)],
    )
    return pl.pallas_call(
        _expert_kernel,
        grid_spec=grid_spec,
        out_shape=jax.ShapeDtypeStruct((p, d), F32),
        compiler_params=_cparams("arbitrary"),
        name="moe_experts",
    )(tile_expert, n_used, xs, wgu, bgu.reshape(depth, e, 1, n2), wd, bd.reshape(depth, e, 1, d))


def _combine_kernel(tab_ref, slotc_ref, w_ref, ys_ref, x_ref, mod_ref, o_ref, yb_ref, sem):
    @pl.when(pl.program_id(0) == 0)
    def _():
        yb_ref[...] = jnp.zeros_like(yb_ref)

    def run_blocks(e, act):
        first = tab_ref[0, 1, e]
        base = tab_ref[0, 2, e]

        def block(off, rows):
            act(pltpu.make_async_copy(
                ys_ref.at[pl.ds(pl.multiple_of(base + off, _RUN), rows)],
                yb_ref.at[pl.ds(pl.multiple_of(first + off, _RUN), rows)], sem))

        _for_each_block(tab_ref[0, 0, e], _block_sizes(_TT // _RUN), block)

    def issue(e, carry):
        run_blocks(e, lambda cp: cp.start())
        return carry

    lax.fori_loop(0, N_EXPERTS, issue, 0)
    used = tab_ref[0, 1, N_EXPERTS - 1] + tab_ref[0, 0, N_EXPERTS - 1] * _RUN
    _for_each_block(used // _RUN, _block_sizes(_SLOTS // _RUN), lambda off, rows: pltpu.make_async_copy(
        ys_ref.at[pl.ds(0, rows)], yb_ref.at[pl.ds(0, rows)], sem).wait())
    rio = lax.broadcasted_iota(jnp.int32, (_SLOTS, 1), 0)
    y = jnp.where(rio < used, yb_ref[...], 0.0).astype(BF16)
    lio = lax.broadcasted_iota(jnp.int32, (_TT, _SLOTS), 1)
    slotc = slotc_ref[...]
    w = w_ref[...]
    gates = jnp.zeros((_TT, _SLOTS), F32)
    for kk in range(TOP_K):
        gates = gates + jnp.where(lio == slotc[:, kk:kk + 1], w[:, kk:kk + 1], 0.0)
    f = _dot(gates.astype(BF16), y)
    o_ref[...] = x_ref[...] + mod_ref[0, 0][5:6] * f


def _combine(tab, slotc, w, ys, xf, mod, ntb, nct):
    n, d = xf.shape
    nt = n // _TT
    seg = lambda s: jnp.where(s % ntb >= nct, 1, 0)
    return pl.pallas_call(
        _combine_kernel,
        grid=(nt,),
        in_specs=[
            pl.BlockSpec((1, 8, _LANES), lambda s: (s, 0, 0), memory_space=pltpu.SMEM),
            pl.BlockSpec((_TT, _LANES), lambda s: (s, 0)),
            pl.BlockSpec((_TT, _LANES), lambda s: (s, 0)),
            pl.BlockSpec(memory_space=pl.ANY),
            pl.BlockSpec((_TT, d), lambda s: (s, 0)),
            pl.BlockSpec((1, 1, N_MOD, d), lambda s: (s // ntb, seg(s), 0, 0)),
        ],
        out_specs=pl.BlockSpec((_TT, d), lambda s: (s, 0)),
        out_shape=jax.ShapeDtypeStruct((n, d), F32),
        scratch_shapes=[pltpu.VMEM((_SLOTS, d), F32), pltpu.SemaphoreType.DMA],
        compiler_params=_cparams("arbitrary"),
        name="moe_combine",
    )(tab, slotc, w, ys, xf, mod)


def _moe(xa, mod, g2, wr, br, layer, wgu, bgu, wd, bd, nct):
    b, t, d = xa.shape
    n = b * t
    ntb = t // _TT
    xf = xa.reshape(n, d)
    nt = n // _TT
    h, slot, slotc, w, tab, tot = _router(xf, mod, g2, wr, br, ntb, nct)
    run_rows = tot[0].astype(jnp.int32)
    padded = (run_rows + _TM - 1) // _TM * _TM
    ends = jnp.cumsum(padded)
    starts = ends - padded
    row = lax.broadcasted_iota(jnp.int32, (8, _LANES), 0)
    tab = tab + jnp.where(row == 2, starts[None, :], 0)[None]
    max_rows = n * TOP_K + nt * N_EXPERTS * (_RUN - 1) + N_EXPERTS * (_TM - 1)
    n_tiles = max_rows // _TM
    lane = lax.broadcasted_iota(jnp.int32, (8, _LANES), 1)
    used_rows = ends[N_EXPERTS - 1]
    free = jnp.where(lane == 0, n_tiles - used_rows // _TM, jnp.where(lane == 1, used_rows, 0))
    tail = jnp.where(row == 0, ((padded - run_rows) // _RUN)[None, :],
                     jnp.where(row == 1, (starts + run_rows)[None, :],
                               jnp.where(row == 2, free, 0)))
    tile_row = jnp.arange(n_tiles, dtype=jnp.int32) * _TM
    tile_expert = jnp.minimum(jnp.sum(tile_row[:, None] >= ends[None, :N_EXPERTS], axis=1),
                              N_EXPERTS - 1).astype(jnp.int32)
    n_used = (ends[N_EXPERTS - 1:N_EXPERTS] // _TM).astype(jnp.int32)
    xs = _dispatch(tab, tail, slot, h, n_tiles * _TM)
    ys = _experts(tile_expert, n_used, xs, layer, wgu, bgu, wd, bd)
    return _combine(tab, slotc, w, ys, xf, mod, ntb, nct).reshape(b, t, d)


def _rope_tables(n_ctx, n_lat):
    rows = n_lat // GRID_W
    row_ids = jnp.repeat(jnp.arange(rows), GRID_W).astype(F32)
    col_ids = jnp.tile(jnp.arange(GRID_W), rows).astype(F32)
    axis_dim = HEAD_DIM // 2
    inv_freq = 1.0 / (ROPE_THETA ** (jnp.arange(0, axis_dim, 2, dtype=F32) / axis_dim))
    ang = jnp.concatenate([row_ids[:, None] * inv_freq, col_ids[:, None] * inv_freq], axis=-1)
    cos = jnp.concatenate([jnp.ones((n_ctx, axis_dim), F32), jnp.cos(ang)], axis=0)
    sin = jnp.concatenate([jnp.zeros((n_ctx, axis_dim), F32), jnp.sin(ang)], axis=0)
    return cos.T, sin.T


def kernel(x, c, ctx, c_ctx, ada_w, ada_b, norm1_g, norm2_g, attn_w_qkv, attn_q_gain, attn_k_gain, attn_w_o, gla_w_in, gla_w_gk2_f, gla_b_gk_f, gla_w_gk2_b, gla_b_gk_b, gla_o_gain, gla_w_o, moe_w_router, moe_b_router, moe_w_gu, moe_b_gu, moe_w_down, moe_b_down):
    b, n_lat, d = x.shape
    n_ctx = ctx.shape[1]
    assert d == D_MODEL and n_ctx % _TT == 0 and n_lat % _TT == 0 and n_lat % GRID_W == 0
    nct = n_ctx // _TT
    depth = ada_w.shape[0]

    r = -(-(b + 1) // 8) * 8
    cond = jnp.zeros((r, d), F32).at[:b].set(c).at[b].set(c_ctx)
    table = _ada_table(cond, ada_w, ada_b).reshape(depth, r, N_MOD, d)
    mods = jnp.stack([jnp.broadcast_to(table[:, b:b + 1], (depth, b, N_MOD, d)), table[:, :b]],
                     axis=2)

    cos_t, sin_t = _rope_tables(n_ctx, n_lat)
    perm = jnp.concatenate([jnp.arange(0, HEAD_DIM, 2), jnp.arange(1, HEAD_DIM, 2)])
    qk_cols = (jnp.arange(N_Q_HEADS + N_KV_HEADS)[:, None] * HEAD_DIM + perm[None, :]).reshape(-1)
    cols = jnp.concatenate([qk_cols, jnp.arange(Q_DIM + KV_DIM, QKV_DIM)])

    xa = jnp.concatenate([ctx, x], axis=1)
    for i in range(depth):
        mod = mods[i]
        j = i // N_MIXERS
        g1 = norm1_g[i].reshape(1, d)
        if i % N_MIXERS == 0:
            wt = attn_w_qkv[j][:, cols].T.astype(BF16)
            gq = attn_q_gain[j][perm].reshape(HEAD_DIM, 1)
            gk = attn_k_gain[j][perm].reshape(HEAD_DIM, 1)
            qt, k, vt = _qkv_project(xa, mod, g1, wt, gq, gk, cos_t, sin_t, nct)
            xa = _attention(qt, k, vt, attn_w_o[j].astype(BF16), xa, mod, nct)
        else:
            n_main = 2 * GLA_DK + 2 * GLA_DV
            wm = gla_w_in[j][:, :n_main].astype(BF16)
            wl = jnp.zeros((d, _LANES), F32).at[:, :2 * GLA_GATE_RANK].set(
                gla_w_in[j][:, n_main:]).astype(BF16)
            w2 = jnp.zeros((_LANES, 2 * GLA_DK), F32)
            w2 = w2.at[:GLA_GATE_RANK, :GLA_DK].set(gla_w_gk2_f[j])
            w2 = w2.at[GLA_GATE_RANK:2 * GLA_GATE_RANK, GLA_DK:].set(gla_w_gk2_b[j])
            b2 = jnp.concatenate([gla_b_gk_f[j], gla_b_gk_b[j]]).reshape(1, 2 * GLA_DK)
            q, k, v, rr, gf, gb = _gla_project(xa, mod, g1, wm, wl, w2, b2, nct)
            o_f, o_b = _gla_scan(q, k, v, gf, gb, nct)
            xa = _gla_readout(o_f, o_b, rr, gla_o_gain[j].reshape(1, GLA_DV_HEAD),
                              gla_w_o[j].astype(BF16), xa, mod, nct)
        wr = jnp.zeros((d, _LANES), F32).at[:, :N_EXPERTS].set(moe_w_router[i])
        br = jnp.full((1, _LANES), _NEG, F32).at[0, :N_EXPERTS].set(moe_b_router[i])
        xa = _moe(xa, mod, norm2_g[i].reshape(1, d), wr, br,
                  i, moe_w_gu, moe_b_gu, moe_w_down, moe_b_down, nct)
    return xa[:, n_ctx:]
```

```python
import functools

import jax
import jax.numpy as jnp
from jax import lax
from jax.experimental import pallas as pl
from jax.experimental.pallas import tpu as pltpu

F32 = jnp.float32
BF16 = jnp.bfloat16

D_MODEL = 1024
DEPTH = 4
GRID_W = 64
N_MIXERS = 2
N_MOD = 6
RMS_EPS = 1e-6
HEAD_DIM = 64
N_Q_HEADS = D_MODEL // HEAD_DIM
N_KV_HEADS = N_Q_HEADS // 4
GQA_GROUP = N_Q_HEADS // N_KV_HEADS
Q_DIM = N_Q_HEADS * HEAD_DIM
KV_DIM = N_KV_HEADS * HEAD_DIM
QKV_DIM = Q_DIM + 2 * KV_DIM
V_ROWS = HEAD_DIM + 16
LOG2E = 1.4426950408889634
ROPE_THETA = 10000.0
GLA_HEADS = 4
GLA_DK = D_MODEL // 2
GLA_DV = D_MODEL
GLA_DK_HEAD = GLA_DK // GLA_HEADS
GLA_DV_HEAD = GLA_DV // GLA_HEADS
GLA_GATE_RANK = 16
GLA_GATE_NORM = 16.0
N_EXPERTS = 32
TOP_K = 4
D_EXPERT = D_MODEL
SWIGLU_LIMIT = 7.0
SWIGLU_ALPHA = 1.702

_LANES = 128
_TT = 256
_KC = 256
_HPI = 2
_GLA_SAFE_LOG_DECAY = 60.0
_TM = 512
_RUN = 8
_SLOTS = _TT * TOP_K + N_EXPERTS * _RUN
_ADA_TN = 1536
_VMEM_LIMIT = 56 * 1024 * 1024
_NEG = -1e30


def _cparams(*sem):
    return pltpu.CompilerParams(dimension_semantics=sem, vmem_limit_bytes=_VMEM_LIMIT)


def _dot(a, b):
    return jnp.dot(a, b, preferred_element_type=F32)


def _dot_nt(a, b):
    return lax.dot_general(a, b, (((1,), (1,)), ((), ())), preferred_element_type=F32)


def _split_bf16(a):
    hi = a.astype(BF16)
    lo = (a - hi.astype(F32)).astype(BF16)
    return hi, lo


def _dot3(a, w):
    ah, al = _split_bf16(a)
    wh, wl = _split_bf16(w)
    return _dot(ah, wh) + _dot(ah, wl) + _dot(al, wh)


def _modulate(x, g, shift, scale):
    ms = jnp.mean(x * x, axis=-1, keepdims=True)
    y = x * lax.rsqrt(ms + RMS_EPS) * g
    return y * (1.0 + scale) + shift


def _sigmoid(x):
    return 1.0 / (1.0 + jnp.exp(-x))


def _ada_kernel(cond_ref, w_ref, b_ref, o_ref):
    c = cond_ref[...]
    o_ref[0] = _dot3(c * _sigmoid(c), w_ref[0]) + b_ref[0]


def _ada_table(cond, ada_w, ada_b):
    depth, d, n = ada_w.shape
    r = cond.shape[0]
    return pl.pallas_call(
        _ada_kernel,
        grid=(depth, n // _ADA_TN),
        in_specs=[
            pl.BlockSpec((r, d), lambda i, j: (0, 0)),
            pl.BlockSpec((1, d, _ADA_TN), lambda i, j: (i, 0, j)),
            pl.BlockSpec((1, 1, _ADA_TN), lambda i, j: (i, 0, j)),
        ],
        out_specs=pl.BlockSpec((1, r, _ADA_TN), lambda i, j: (i, 0, j)),
        out_shape=jax.ShapeDtypeStruct((depth, r, n), F32),
        compiler_params=_cparams("parallel", "parallel"),
        name="ada_table",
    )(cond, ada_w, ada_b.reshape(depth, 1, n))


def _qkv_kernel(x_ref, mod_ref, g_ref, wt_ref, gq_ref, gk_ref, cos_ref, sin_ref,
                qt_ref, k_ref, vt_ref):
    mod = mod_ref[0, 0]
    h = _modulate(x_ref[0], g_ref[...], mod[0:1], mod[1:2])
    ht = h.T.astype(BF16)
    qkvt = _dot(wt_ref[...], ht)
    c = cos_ref[...]
    s = sin_ref[...]
    half = HEAD_DIM // 2

    def norm_rope(blk, gain, scale):
        ms = jnp.mean(blk * blk, axis=0, keepdims=True)
        n = blk * lax.rsqrt(ms + RMS_EPS) * gain
        x1 = n[0:half]
        x2 = n[half:HEAD_DIM]
        return jnp.concatenate([x1 * c - x2 * s, x1 * s + x2 * c], axis=0) * scale

    gq = gq_ref[...]
    gk = gk_ref[...]
    for hh in range(N_Q_HEADS):
        r0 = hh * HEAD_DIM
        qt_ref[0, r0:r0 + HEAD_DIM, :] = norm_rope(
            qkvt[r0:r0 + HEAD_DIM], gq, LOG2E * HEAD_DIM ** -0.5).astype(BF16)
    kt = jnp.concatenate(
        [norm_rope(qkvt[Q_DIM + j * HEAD_DIM:Q_DIM + (j + 1) * HEAD_DIM], gk, 1.0)
         for j in range(N_KV_HEADS)], axis=0)
    k_ref[0] = kt.T.astype(BF16)
    ones = jnp.ones((V_ROWS - HEAD_DIM, vt_ref.shape[2]), BF16)
    for j in range(N_KV_HEADS):
        v0 = Q_DIM + KV_DIM + j * HEAD_DIM
        vt_ref[0, j * V_ROWS:j * V_ROWS + HEAD_DIM, :] = qkvt[v0:v0 + HEAD_DIM].astype(BF16)
        vt_ref[0, j * V_ROWS + HEAD_DIM:(j + 1) * V_ROWS, :] = ones


def _qkv_project(xa, mod, g, wt, gq, gk, cos_t, sin_t, nct):
    b, t, d = xa.shape
    nt = t // _TT
    seg = lambda j: jnp.where(j >= nct, 1, 0)
    return pl.pallas_call(
        _qkv_kernel,
        grid=(b, nt),
        in_specs=[
            pl.BlockSpec((1, _TT, d), lambda i, j: (i, j, 0)),
            pl.BlockSpec((1, 1, N_MOD, d), lambda i, j: (i, seg(j), 0, 0)),
            pl.BlockSpec((1, d), lambda i, j: (0, 0)),
            pl.BlockSpec((QKV_DIM, d), lambda i, j: (0, 0)),
            pl.BlockSpec((HEAD_DIM, 1), lambda i, j: (0, 0)),
            pl.BlockSpec((HEAD_DIM, 1), lambda i, j: (0, 0)),
            pl.BlockSpec((HEAD_DIM // 2, _TT), lambda i, j: (0, j)),
            pl.BlockSpec((HEAD_DIM // 2, _TT), lambda i, j: (0, j)),
        ],
        out_specs=[
            pl.BlockSpec((1, Q_DIM, _TT), lambda i, j: (i, 0, j)),
            pl.BlockSpec((1, _TT, KV_DIM), lambda i, j: (i, j, 0)),
            pl.BlockSpec((1, N_KV_HEADS * V_ROWS, _TT), lambda i, j: (i, 0, j)),
        ],
        out_shape=[
            jax.ShapeDtypeStruct((b, Q_DIM, t), BF16),
            jax.ShapeDtypeStruct((b, t, KV_DIM), BF16),
            jax.ShapeDtypeStruct((b, N_KV_HEADS * V_ROWS, t), BF16),
        ],
        compiler_params=_cparams("parallel", "parallel"),
        name="attn_qkv",
    )(xa, mod, g, wt, gq, gk, cos_t, sin_t)


def _attn_kernel(qt_ref, k_ref, vt_ref, wo_ref, x_ref, mod_ref, o_ref, qe_ref, ot_ref,
                 *, nct, n_ctx):
    j = pl.program_id(1)
    t_all = k_ref.shape[1]

    def heads(nk):
        def head_pair(hp, carry):
            kvh = hp // (GQA_GROUP // _HPI)
            k0 = pl.multiple_of(kvh * HEAD_DIM, HEAD_DIM)
            v0 = pl.multiple_of(kvh * V_ROWS, 16)
            for u in range(_HPI):
                q0 = pl.multiple_of((_HPI * hp + u) * HEAD_DIM, HEAD_DIM)
                qe_ref[u] = jnp.zeros(qe_ref.shape[1:], BF16)
                qe_ref[u, pl.ds(k0, HEAD_DIM), :] = qt_ref[0, pl.ds(q0, HEAD_DIM), :]
            bounds = list(range(0, nk, _KC)) + [nk]
            chunks = list(zip(bounds[:-1], bounds[1:]))

            st = [_dot(k_ref[0, 0:nk, :], qe_ref[u]) for u in range(_HPI)]
            acc = [jnp.zeros((V_ROWS, _TT), F32) for _ in range(_HPI)]
            m_run = [jnp.full((1, _TT), _NEG, F32) for _ in range(_HPI)]
            for lo, hi in chunks:
                for u in range(_HPI):
                    s_cur = st[u][lo:hi]
                    m_new = jnp.maximum(m_run[u], jnp.max(s_cur, axis=0, keepdims=True))
                    p = jnp.exp2(s_cur - m_new).astype(BF16)
                    pv = _dot(vt_ref[0, pl.ds(v0, V_ROWS), lo:hi], p)
                    acc[u] = acc[u] * jnp.exp2(m_run[u] - m_new) + pv
                    m_run[u] = m_new
            for u in range(_HPI):
                q0 = pl.multiple_of((_HPI * hp + u) * HEAD_DIM, HEAD_DIM)
                ot_ref[pl.ds(q0, HEAD_DIM), :] = acc[u][0:HEAD_DIM] / acc[u][HEAD_DIM:HEAD_DIM + 1]
            return carry

        lax.fori_loop(0, N_Q_HEADS // _HPI, head_pair, 0)

    @pl.when(j < nct)
    def _():
        heads(n_ctx)

    @pl.when(j >= nct)
    def _():
        heads(t_all)

    att = ot_ref[...].T.astype(BF16)
    out = _dot(att, wo_ref[...])
    o_ref[0] = x_ref[0] + mod_ref[0, 0][2:3] * out


def _attention(qt, k, vt, wo, xa, mod, nct):
    b, t, d = xa.shape
    nt = t // _TT
    seg = lambda j: jnp.where(j >= nct, 1, 0)
    kern = functools.partial(_attn_kernel, nct=nct, n_ctx=nct * _TT)
    return pl.pallas_call(
        kern,
        grid=(b, nt),
        in_specs=[
            pl.BlockSpec((1, Q_DIM, _TT), lambda i, j: (i, 0, j)),
            pl.BlockSpec((1, t, KV_DIM), lambda i, j: (i, 0, 0)),
            pl.BlockSpec((1, N_KV_HEADS * V_ROWS, t), lambda i, j: (i, 0, 0)),
            pl.BlockSpec((Q_DIM, d), lambda i, j: (0, 0)),
            pl.BlockSpec((1, _TT, d), lambda i, j: (i, j, 0)),
            pl.BlockSpec((1, 1, N_MOD, d), lambda i, j: (i, seg(j), 0, 0)),
        ],
        out_specs=pl.BlockSpec((1, _TT, d), lambda i, j: (i, j, 0)),
        out_shape=jax.ShapeDtypeStruct((b, t, d), F32),
        scratch_shapes=[pltpu.VMEM((_HPI, KV_DIM, _TT), BF16), pltpu.VMEM((Q_DIM, _TT), F32)],
        compiler_params=_cparams("parallel", "arbitrary"),
        name="attn_core",
    )(qt, k, vt, wo, xa, mod)


def _gla_in_kernel(x_ref, mod_ref, g_ref, wm_ref, wl_ref, w2_ref, b2_ref,
                   q_ref, k_ref, v_ref, r_ref, gf_ref, gb_ref):
    mod = mod_ref[0, 0]
    h = _modulate(x_ref[0], g_ref[...], mod[0:1], mod[1:2]).astype(BF16)
    main = _dot(h, wm_ref[...])
    q_ref[0] = (main[:, 0:GLA_DK] * GLA_DK_HEAD ** -0.5).astype(BF16)
    k_ref[0] = main[:, GLA_DK:2 * GLA_DK].astype(BF16)
    v_ref[0] = main[:, 2 * GLA_DK:2 * GLA_DK + GLA_DV].astype(BF16)
    r_ref[0] = main[:, 2 * GLA_DK + GLA_DV:].astype(BF16)
    low = _dot(h, wl_ref[...])
    pre = _dot3(low, w2_ref[...]) + b2_ref[...]
    logsig = jnp.minimum(pre, 0.0) - jnp.log(1.0 + jnp.exp(-jnp.abs(pre)))
    gate = logsig / GLA_GATE_NORM
    gf_ref[0] = gate[:, 0:GLA_DK]
    gb_ref[0] = gate[:, GLA_DK:]


def _gla_project(xa, mod, g, wm, wl, w2, b2, nct):
    b, t, d = xa.shape
    nt = t // _TT
    seg = lambda j: jnp.where(j >= nct, 1, 0)
    tile = lambda n: pl.BlockSpec((1, _TT, n), lambda i, j: (i, j, 0))
    full = lambda a: pl.BlockSpec(a.shape, lambda i, j: (0,) * a.ndim)
    return pl.pallas_call(
        _gla_in_kernel,
        grid=(b, nt),
        in_specs=[
            tile(d),
            pl.BlockSpec((1, 1, N_MOD, d), lambda i, j: (i, seg(j), 0, 0)),
            full(g), full(wm), full(wl), full(w2), full(b2),
        ],
        out_specs=[tile(GLA_DK), tile(GLA_DK), tile(GLA_DV), tile(GLA_DV),
                   tile(GLA_DK), tile(GLA_DK)],
        out_shape=[
            jax.ShapeDtypeStruct((b, t, GLA_DK), BF16),
            jax.ShapeDtypeStruct((b, t, GLA_DK), BF16),
            jax.ShapeDtypeStruct((b, t, GLA_DV), BF16),
            jax.ShapeDtypeStruct((b, t, GLA_DV), BF16),
            jax.ShapeDtypeStruct((b, t, GLA_DK), F32),
            jax.ShapeDtypeStruct((b, t, GLA_DK), F32),
        ],
        compiler_params=_cparams("parallel", "parallel"),
        name="gla_in",
    )(xa, mod, g, wm, wl, w2, b2)


def _gla_scan_kernel(qf_ref, kf_ref, vf_ref, gf_ref, qb_ref, kb_ref, vb_ref, gb_ref,
                     tril_ref, triu_ref, of_ref, ob_ref, st_ref, a_ref, kx_ref, bx_ref):
    @pl.when(pl.program_id(1) == 0)
    def _():
        st_ref[...] = jnp.zeros_like(st_ref)

    chains = []
    for h in range(GLA_HEADS):
        kcols = slice(h * GLA_DK_HEAD, (h + 1) * GLA_DK_HEAD)
        vcols = slice(h * GLA_DV_HEAD, (h + 1) * GLA_DV_HEAD)
        chains.append((qf_ref, kf_ref, vf_ref, gf_ref, tril_ref, of_ref, kcols, vcols, False))
        chains.append((qb_ref, kb_ref, vb_ref, gb_ref, triu_ref, ob_ref, kcols, vcols, True))

    work = []
    for ci, (q_ref, k_ref, v_ref, g_ref, tri_ref, o_ref, kcols, vcols, reverse) in enumerate(chains):
        tri = tri_ref[...]
        gh, gl = _split_bf16(g_ref[0, :, kcols])
        bcum = _dot(tri, gh) + _dot(tri, gl)
        b_end = bcum[0:1] if reverse else bcum[_TT - 1:_TT]
        q = q_ref[0, :, kcols].astype(F32)
        k = k_ref[0, :, kcols].astype(F32)
        qd = (q * jnp.exp(bcum)).astype(BF16)
        kd = (k * jnp.exp(b_end - bcum)).astype(BF16)
        a_ref[ci] = _dot_nt(qd, (k * jnp.exp(-bcum)).astype(BF16))
        work.append((q, k, bcum, b_end, qd, kd))

    for ci, (q, k, bcum, b_end, qd, kd) in enumerate(work):
        @pl.when(jnp.max(-b_end) > _GLA_SAFE_LOG_DECAY)
        def _():
            kx_ref[...] = k
            bx_ref[...] = bcum
            lane = lax.broadcasted_iota(jnp.int32, (_TT, _TT), 1)

            def column(j, carry):
                decay = jnp.exp(jnp.minimum(bcum - bx_ref[pl.ds(j, 1), :], 0.0))
                col = jnp.sum(q * decay * kx_ref[pl.ds(j, 1), :], axis=1, keepdims=True)
                a_ref[ci] = jnp.where(lane == j, col, a_ref[ci])
                return carry

            lax.fori_loop(0, _TT, column, 0)

    for ci, (q_ref, k_ref, v_ref, g_ref, tri_ref, o_ref, kcols, vcols, reverse) in enumerate(chains):
        q, k, bcum, b_end, qd, kd = work[ci]
        v = v_ref[0, :, vcols]
        a = jnp.where(tri_ref[...] > 0, a_ref[ci], 0.0).astype(BF16)
        st = st_ref[ci]
        o_ref[0, :, vcols] = _dot_nt(qd, st.astype(BF16)) + _dot(a, v)
        vt = v.astype(F32).T.astype(BF16)
        st_ref[ci] = st * jnp.exp(b_end) + _dot(vt, kd)


def _gla_scan(q, k, v, gf, gb, nct):
    b, t, _ = q.shape
    nb = t // _TT

    def rev(s):
        return jnp.where(s < nct, nct - 1 - s, nb - 1 - (s - nct))

    kspec = pl.BlockSpec((1, _TT, GLA_DK), lambda i, s: (i, s, 0))
    vspec = pl.BlockSpec((1, _TT, GLA_DV), lambda i, s: (i, s, 0))
    kspec_r = pl.BlockSpec((1, _TT, GLA_DK), lambda i, s: (i, rev(s), 0))
    vspec_r = pl.BlockSpec((1, _TT, GLA_DV), lambda i, s: (i, rev(s), 0))
    tspec = pl.BlockSpec((_TT, _TT), lambda i, s: (0, 0))
    row = lax.broadcasted_iota(jnp.int32, (_TT, _TT), 0)
    col = lax.broadcasted_iota(jnp.int32, (_TT, _TT), 1)
    n_chains = 2 * GLA_HEADS
    return pl.pallas_call(
        _gla_scan_kernel,
        grid=(b, nb),
        in_specs=[kspec, kspec, vspec, kspec, kspec_r, kspec_r, vspec_r, kspec_r, tspec, tspec],
        out_specs=[vspec, vspec_r],
        out_shape=[jax.ShapeDtypeStruct((b, t, GLA_DV), F32)] * 2,
        scratch_shapes=[pltpu.VMEM((n_chains, GLA_DV_HEAD, GLA_DK_HEAD), F32),
                        pltpu.VMEM((n_chains, _TT, _TT), F32),
                        pltpu.VMEM((_TT, GLA_DK_HEAD), F32),
                        pltpu.VMEM((_TT, GLA_DK_HEAD), F32)],
        compiler_params=_cparams("parallel", "arbitrary"),
        name="gla_scan",
    )(q, k, v, gf, q, k, v, gb, (col <= row).astype(BF16), (col >= row).astype(BF16))


def _gla_out_kernel(of_ref, ob_ref, r_ref, og_ref, wo_ref, x_ref, mod_ref, o_ref):
    o = of_ref[0] + ob_ref[0]
    og = og_ref[...]
    parts = []
    for h in range(GLA_HEADS):
        oh = o[:, h * GLA_DV_HEAD:(h + 1) * GLA_DV_HEAD]
        ms = jnp.mean(oh * oh, axis=-1, keepdims=True)
        parts.append(oh * lax.rsqrt(ms + RMS_EPS) * og)
    r = r_ref[0].astype(F32)
    y = jnp.concatenate(parts, axis=1) * (r * _sigmoid(r))
    out = _dot(y.astype(BF16), wo_ref[...])
    o_ref[0] = x_ref[0] + mod_ref[0, 0][2:3] * out


def _gla_readout(o_f, o_b, r, og, wo, xa, mod, nct):
    b, t, d = xa.shape
    nt = t // _TT
    seg = lambda j: jnp.where(j >= nct, 1, 0)
    tile = lambda n: pl.BlockSpec((1, _TT, n), lambda i, j: (i, j, 0))
    return pl.pallas_call(
        _gla_out_kernel,
        grid=(b, nt),
        in_specs=[
            tile(GLA_DV), tile(GLA_DV), tile(GLA_DV),
            pl.BlockSpec((1, GLA_DV_HEAD), lambda i, j: (0, 0)),
            pl.BlockSpec((GLA_DV, d), lambda i, j: (0, 0)),
            tile(d),
            pl.BlockSpec((1, 1, N_MOD, d), lambda i, j: (i, seg(j), 0, 0)),
        ],
        out_specs=tile(d),
        out_shape=jax.ShapeDtypeStruct((b, t, d), F32),
        compiler_params=_cparams("parallel", "parallel"),
        name="gla_out",
    )(o_f, o_b, r, og, wo, xa, mod)


def _pad_run(count):
    return jnp.floor((count + (_RUN - 1)) * (1.0 / _RUN)) * _RUN


def _router_kernel(x_ref, mod_ref, g_ref, wr_ref, br_ref, h_ref, slot_ref, slotc_ref, w_ref,
                   tab_ref, tot_ref, carry_ref):
    i = pl.program_id(0)

    @pl.when(i == 0)
    def _():
        carry_ref[...] = jnp.zeros_like(carry_ref)

    mod = mod_ref[0, 0]
    h = _modulate(x_ref[...], g_ref[...], mod[3:4], mod[4:5])
    h_ref[...] = h
    logits = _dot3(h, wr_ref[...]) + br_ref[...]
    lt = logits.T[0:N_EXPERTS]
    eio = lax.broadcasted_iota(jnp.int32, lt.shape, 0)
    vals, hots = [], []
    for kk in range(TOP_K):
        m = jnp.max(lt, axis=0, keepdims=True)
        ik = jnp.min(jnp.where(lt == m, eio, N_EXPERTS), axis=0, keepdims=True)
        hot = eio == ik
        lt = jnp.where(hot, -jnp.inf, lt)
        vals.append(m)
        hots.append(hot)
    es = [jnp.exp(v - vals[0]) for v in vals]
    denom = es[0] + es[1] + es[2] + es[3]
    member = jnp.zeros(lt.shape, F32)
    for hot in hots:
        member = member + hot.astype(F32)
    member_bf = member.astype(BF16)

    r_t = lax.broadcasted_iota(jnp.int32, (_TT, _TT), 0)
    c_t = lax.broadcasted_iota(jnp.int32, (_TT, _TT), 1)
    before = (r_t < c_t).astype(BF16)
    in_run = _dot(member_bf, before)
    run_col = _pad_run(jnp.sum(member, axis=1, keepdims=True))
    r_e = lax.broadcasted_iota(jnp.int32, (N_EXPERTS, N_EXPERTS), 0)
    c_e = lax.broadcasted_iota(jnp.int32, (N_EXPERTS, N_EXPERTS), 1)
    first_col = _dot((c_e < r_e).astype(BF16),
                     jnp.broadcast_to(run_col, (N_EXPERTS, _LANES)).astype(BF16))[:, 0:1]
    slot = first_col + in_run
    rsub = lax.broadcasted_iota(jnp.int32, (_LANES, _TT), 0)
    slot_rows = jnp.zeros((_LANES, _TT), F32)
    w_rows = jnp.zeros((_LANES, _TT), F32)
    for kk in range(TOP_K):
        sk = jnp.sum(jnp.where(hots[kk], slot, 0.0), axis=0, keepdims=True)
        slot_rows = jnp.where(rsub == kk, sk, slot_rows)
        w_rows = jnp.where(rsub == kk, es[kk] / denom, w_rows)
    slot_ref[...] = slot_rows[0:8].astype(jnp.int32)
    slotc_ref[...] = slot_rows.T.astype(jnp.int32)
    w_ref[...] = w_rows.T

    member_pad = jnp.concatenate(
        [member_bf, jnp.zeros((_LANES - N_EXPERTS, _TT), BF16)], axis=0)
    run_row = _pad_run(_dot_nt(jnp.ones((8, _TT), BF16), member_pad))
    r_l = lax.broadcasted_iota(jnp.int32, (_LANES, _LANES), 0)
    c_l = lax.broadcasted_iota(jnp.int32, (_LANES, _LANES), 1)
    first_row = _dot(run_row.astype(BF16), (r_l < c_l).astype(BF16))
    earlier = carry_ref[...]
    tsub = lax.broadcasted_iota(jnp.int32, (8, _LANES), 0)
    tab = jnp.where(tsub == 0, run_row * (1.0 / _RUN),
                    jnp.where(tsub == 1, first_row, jnp.where(tsub == 2, earlier, 0.0)))
    tab_ref[0] = tab.astype(jnp.int32)
    carry_ref[...] = earlier + run_row
    tot_ref[...] = earlier + run_row


def _router(xf, mod, g, wr, br, ntb, nct):
    n, d = xf.shape
    nt = n // _TT
    seg = lambda s: jnp.where(s % ntb >= nct, 1, 0)
    return pl.pallas_call(
        _router_kernel,
        grid=(nt,),
        in_specs=[
            pl.BlockSpec((_TT, d), lambda s: (s, 0)),
            pl.BlockSpec((1, 1, N_MOD, d), lambda s: (s // ntb, seg(s), 0, 0)),
            pl.BlockSpec((1, d), lambda s: (0, 0)),
            pl.BlockSpec((d, _LANES), lambda s: (0, 0)),
            pl.BlockSpec((1, _LANES), lambda s: (0, 0)),
        ],
        out_specs=[
            pl.BlockSpec((_TT, d), lambda s: (s, 0)),
            pl.BlockSpec((8, _TT), lambda s: (0, s)),
            pl.BlockSpec((_TT, _LANES), lambda s: (s, 0)),
            pl.BlockSpec((_TT, _LANES), lambda s: (s, 0)),
            pl.BlockSpec((1, 8, _LANES), lambda s: (s, 0, 0)),
            pl.BlockSpec((8, _LANES), lambda s: (0, 0)),
        ],
        out_shape=[
            jax.ShapeDtypeStruct((n, d), F32),
            jax.ShapeDtypeStruct((8, n), jnp.int32),
            jax.ShapeDtypeStruct((n, _LANES), jnp.int32),
            jax.ShapeDtypeStruct((n, _LANES), F32),
            jax.ShapeDtypeStruct((nt, 8, _LANES), jnp.int32),
            jax.ShapeDtypeStruct((8, _LANES), F32),
        ],
        scratch_shapes=[pltpu.VMEM((8, _LANES), F32)],
        compiler_params=_cparams("arbitrary"),
        name="moe_router",
    )(xf, mod, g, wr, br)


def _block_sizes(max_groups):
    return tuple(1 << b for b in range(max_groups.bit_length() - 1, -1, -1))


def _for_each_block(groups, sizes, fn):
    for size in sizes:
        shift = size.bit_length()
        done = ((groups >> shift) << shift) * _RUN

        @pl.when((groups & size) != 0)
        def _():
            fn(done, size * _RUN)


def _dispatch_kernel(tab_ref, tail_ref, slot_ref, h_ref, xs_ref, xt_ref, used_ref, sem):
    s = pl.program_id(0)
    last = pl.num_programs(0) - 1
    buf = s % 2

    def wait_rows(bi, groups):
        _for_each_block(groups, _block_sizes(_SLOTS // _RUN), lambda off, rows: pltpu.make_async_copy(
            xt_ref.at[bi, pl.ds(0, rows)], xs_ref.at[pl.ds(0, rows)], sem.at[bi]).wait())

    @pl.when(s >= 2)
    def _():
        wait_rows(buf, used_ref[buf])

    slots = slot_ref[...]
    sio = lax.broadcasted_iota(jnp.int32, (_SLOTS, _TT), 0)
    perm = jnp.zeros((_SLOTS, _TT), F32)
    for kk in range(TOP_K):
        perm = perm + (sio == slots[kk:kk + 1]).astype(F32)
    xt_ref[buf] = _dot(perm.astype(BF16), h_ref[...].astype(BF16))

    def issue(e, carry):
        first = tab_ref[0, 1, e]
        base = tab_ref[0, 2, e]

        def block(off, rows):
            pltpu.make_async_copy(
                xt_ref.at[buf, pl.ds(pl.multiple_of(first + off, _RUN), rows)],
                xs_ref.at[pl.ds(pl.multiple_of(base + off, _RUN), rows)], sem.at[buf]).start()

        _for_each_block(tab_ref[0, 0, e], _block_sizes(_TT // _RUN), block)
        return carry

    lax.fori_loop(0, N_EXPERTS, issue, 0)
    used = tab_ref[0, 1, N_EXPERTS - 1] // _RUN + tab_ref[0, 0, N_EXPERTS - 1]
    used_ref[buf] = used

    @pl.when(s == last)
    def _():
        @pl.when(s >= 1)
        def _():
            wait_rows(1 - buf, used_ref[1 - buf])

        wait_rows(buf, used)
        xt_ref[0, 0:_TM, :] = jnp.zeros((_TM, xt_ref.shape[2]), F32)

        def tail_blocks(e, act):
            base = tail_ref[1, e]

            def block(off, rows):
                act(pltpu.make_async_copy(
                    xt_ref.at[0, pl.ds(0, rows)],
                    xs_ref.at[pl.ds(pl.multiple_of(base + off, _RUN), rows)], sem.at[0]))

            _for_each_block(tail_ref[0, e], _block_sizes(_TM // _RUN - 1), block)

        def issue_tail(e, carry):
            tail_blocks(e, lambda cp: cp.start())
            return carry

        def drain_tail(e, carry):
            tail_blocks(e, lambda cp: cp.wait())
            return carry

        lax.fori_loop(0, N_EXPERTS, issue_tail, 0)
        lax.fori_loop(0, N_EXPERTS, drain_tail, 0)

        def free_tile(i):
            row0 = pl.multiple_of(tail_ref[2, 1] + i * _TM, _TM)
            return pltpu.make_async_copy(xt_ref.at[0, pl.ds(0, _TM)], xs_ref.at[pl.ds(row0, _TM)],
                                         sem.at[0])

        def issue_free(i, carry):
            free_tile(i).start()
            return carry

        def drain_free(i, carry):
            free_tile(i).wait()
            return carry

        lax.fori_loop(0, tail_ref[2, 0], issue_free, 0)
        lax.fori_loop(0, tail_ref[2, 0], drain_free, 0)


def _dispatch(tab, tail, slot, h, rows):
    n, d = h.shape
    nt = n // _TT
    return pl.pallas_call(
        _dispatch_kernel,
        grid=(nt,),
        in_specs=[
            pl.BlockSpec((1, 8, _LANES), lambda s: (s, 0, 0), memory_space=pltpu.SMEM),
            pl.BlockSpec((8, _LANES), lambda s: (0, 0), memory_space=pltpu.SMEM),
            pl.BlockSpec((8, _TT), lambda s: (0, s)),
            pl.BlockSpec((_TT, d), lambda s: (s, 0)),
        ],
        out_specs=pl.BlockSpec(memory_space=pl.ANY),
        out_shape=jax.ShapeDtypeStruct((rows, d), F32),
        scratch_shapes=[pltpu.VMEM((2, _SLOTS, d), F32), pltpu.SMEM((2,), jnp.int32),
                        pltpu.SemaphoreType.DMA((2,))],
        compiler_params=_cparams("arbitrary"),
        name="moe_dispatch",
    )(tab, tail, slot, h)


def _expert_kernel(te_ref, nu_ref, xs_ref, wgu_ref, bgu_ref, wd_ref, bd_ref, ys_ref,
                   wgu_bf_ref, wd_bf_ref):
    t = pl.program_id(0)
    live = t < nu_ref[0]
    new_expert = jnp.logical_or(t == 0, te_ref[t] != te_ref[jnp.maximum(t - 1, 0)])

    @pl.when(jnp.logical_and(live, new_expert))
    def _():
        wgu_bf_ref[...] = wgu_ref[0, 0].astype(BF16)
        wd_bf_ref[...] = wd_ref[0, 0].astype(BF16)

    @pl.when(live)
    def _():
        x = xs_ref[...].astype(BF16)
        gu = _dot(x, wgu_bf_ref[...]) + bgu_ref[0, 0]
        gate = jnp.minimum(gu[:, 0:D_EXPERT], SWIGLU_LIMIT)
        up = jnp.clip(gu[:, D_EXPERT:], -SWIGLU_LIMIT, SWIGLU_LIMIT)
        act = (up + 1.0) * (gate * _sigmoid(SWIGLU_ALPHA * gate))
        ys_ref[...] = _dot(act.astype(BF16), wd_bf_ref[...]) + bd_ref[0, 0]

    @pl.when(jnp.logical_not(live))
    def _():
        ys_ref[...] = jnp.zeros_like(ys_ref)


def _experts(tile_expert, n_used, xs, layer, wgu, bgu, wd, bd):
    p, d = xs.shape
    nt = p // _TM
    depth, e, _, n2 = wgu.shape
    grid_spec = pltpu.PrefetchScalarGridSpec(
        num_scalar_prefetch=2,
        grid=(nt,),
        in_specs=[
            pl.BlockSpec((_TM, d), lambda t, te, nu: (jnp.minimum(t, nu[0] - 1), 0)),
            pl.BlockSpec((1, 1, d, n2), lambda t, te, nu: (layer, te[t], 0, 0)),
            pl.BlockSpec((1, 1, 1, n2), lambda t, te, nu: (layer, te[t], 0, 0)),
            pl.BlockSpec((1, 1, D_EXPERT, d), lambda t, te, nu: (layer, te[t], 0, 0)),
            pl.BlockSpec((1, 1, 1, d), lambda t, te, nu: (layer, te[t], 0, 0)),
        ],
        out_specs=pl.BlockSpec((_TM, d), lambda t, te, nu: (t, 0)),
        scratch_shapes=[pltpu.VMEM((d, n2), BF16), pltpu.VMEM((D_EXPERT, d), BF16)],
    )
    return pl.pallas_call(
        _expert_kernel,
        grid_spec=grid_spec,
        out_shape=jax.ShapeDtypeStruct((p, d), F32),
        compiler_params=_cparams("arbitrary"),
        name="moe_experts",
    )(tile_expert, n_used, xs, wgu, bgu.reshape(depth, e, 1, n2), wd, bd.reshape(depth, e, 1, d))


def _combine_kernel(tab_ref, tabn_ref, slotc_ref, w_ref, ys_ref, x_ref, mod_ref, o_ref, yb_ref, sem):
    s = pl.program_id(0)
    buf = s % 2

    def issue_runs(t_ref, bi):
        def issue(e, carry):
            first = t_ref[0, 1, e]
            base = t_ref[0, 2, e]

            def block(off, rows):
                pltpu.make_async_copy(
                    ys_ref.at[pl.ds(pl.multiple_of(base + off, _RUN), rows)],
                    yb_ref.at[bi, pl.ds(pl.multiple_of(first + off, _RUN), rows)], sem.at[bi]).start()

            _for_each_block(t_ref[0, 0, e], _block_sizes(_TT // _RUN), block)
            return carry

        lax.fori_loop(0, N_EXPERTS, issue, 0)

    @pl.when(s == 0)
    def _():
        yb_ref[...] = jnp.zeros_like(yb_ref)
        issue_runs(tab_ref, 0)

    @pl.when(s + 1 < pl.num_programs(0))
    def _():
        issue_runs(tabn_ref, 1 - buf)

    used = tab_ref[0, 1, N_EXPERTS - 1] + tab_ref[0, 0, N_EXPERTS - 1] * _RUN
    _for_each_block(used // _RUN, _block_sizes(_SLOTS // _RUN), lambda off, rows: pltpu.make_async_copy(
        ys_ref.at[pl.ds(0, rows)], yb_ref.at[buf, pl.ds(0, rows)], sem.at[buf]).wait())
    rio = lax.broadcasted_iota(jnp.int32, (_SLOTS, 1), 0)
    y = jnp.where(rio < used, yb_ref[buf], 0.0).astype(BF16)
    lio = lax.broadcasted_iota(jnp.int32, (_TT, _SLOTS), 1)
    slotc = slotc_ref[...]
    w = w_ref[...]
    gates = jnp.zeros((_TT, _SLOTS), F32)
    for kk in range(TOP_K):
        gates = gates + jnp.where(lio == slotc[:, kk:kk + 1], w[:, kk:kk + 1], 0.0)
    f = _dot(gates.astype(BF16), y)
    o_ref[...] = x_ref[...] + mod_ref[0, 0][5:6] * f


def _combine(tab, slotc, w, ys, xf, mod, ntb, nct):
    n, d = xf.shape
    nt = n // _TT
    seg = lambda s: jnp.where(s % ntb >= nct, 1, 0)
    return pl.pallas_call(
        _combine_kernel,
        grid=(nt,),
        in_specs=[
            pl.BlockSpec((1, 8, _LANES), lambda s: (s, 0, 0), memory_space=pltpu.SMEM),
            pl.BlockSpec((1, 8, _LANES), lambda s: (jnp.minimum(s + 1, nt - 1), 0, 0),
                         memory_space=pltpu.SMEM),
            pl.BlockSpec((_TT, _LANES), lambda s: (s, 0)),
            pl.BlockSpec((_TT, _LANES), lambda s: (s, 0)),
            pl.BlockSpec(memory_space=pl.ANY),
            pl.BlockSpec((_TT, d), lambda s: (s, 0)),
            pl.BlockSpec((1, 1, N_MOD, d), lambda s: (s // ntb, seg(s), 0, 0)),
        ],
        out_specs=pl.BlockSpec((_TT, d), lambda s: (s, 0)),
        out_shape=jax.ShapeDtypeStruct((n, d), F32),
        scratch_shapes=[pltpu.VMEM((2, _SLOTS, d), F32), pltpu.SemaphoreType.DMA((2,))],
        compiler_params=_cparams("arbitrary"),
        name="moe_combine",
    )(tab, tab, slotc, w, ys, xf, mod)


def _moe(xa, mod, g2, wr, br, layer, wgu, bgu, wd, bd, nct):
    b, t, d = xa.shape
    n = b * t
    ntb = t // _TT
    xf = xa.reshape(n, d)
    nt = n // _TT
    h, slot, slotc, w, tab, tot = _router(xf, mod, g2, wr, br, ntb, nct)
    run_rows = tot[0].astype(jnp.int32)
    padded = (run_rows + _TM - 1) // _TM * _TM
    ends = jnp.cumsum(padded)
    starts = ends - padded
    row = lax.broadcasted_iota(jnp.int32, (8, _LANES), 0)
    tab = tab + jnp.where(row == 2, starts[None, :], 0)[None]
    max_rows = n * TOP_K + nt * N_EXPERTS * (_RUN - 1) + N_EXPERTS * (_TM - 1)
    n_tiles = max_rows // _TM
    lane = lax.broadcasted_iota(jnp.int32, (8, _LANES), 1)
    used_rows = ends[N_EXPERTS - 1]
    free = jnp.where(lane == 0, n_tiles - used_rows // _TM, jnp.where(lane == 1, used_rows, 0))
    tail = jnp.where(row == 0, ((padded - run_rows) // _RUN)[None, :],
                     jnp.where(row == 1, (starts + run_rows)[None, :],
                               jnp.where(row == 2, free, 0)))
    tile_row = jnp.arange(n_tiles, dtype=jnp.int32) * _TM
    tile_expert = jnp.minimum(jnp.sum(tile_row[:, None] >= ends[None, :N_EXPERTS], axis=1),
                              N_EXPERTS - 1).astype(jnp.int32)
    n_used = (ends[N_EXPERTS - 1:N_EXPERTS] // _TM).astype(jnp.int32)
    xs = _dispatch(tab, tail, slot, h, n_tiles * _TM)
    ys = _experts(tile_expert, n_used, xs, layer, wgu, bgu, wd, bd)
    return _combine(tab, slotc, w, ys, xf, mod, ntb, nct).reshape(b, t, d)


def _rope_tables(n_ctx, n_lat):
    rows = n_lat // GRID_W
    row_ids = jnp.repeat(jnp.arange(rows), GRID_W).astype(F32)
    col_ids = jnp.tile(jnp.arange(GRID_W), rows).astype(F32)
    axis_dim = HEAD_DIM // 2
    inv_freq = 1.0 / (ROPE_THETA ** (jnp.arange(0, axis_dim, 2, dtype=F32) / axis_dim))
    ang = jnp.concatenate([row_ids[:, None] * inv_freq, col_ids[:, None] * inv_freq], axis=-1)
    cos = jnp.concatenate([jnp.ones((n_ctx, axis_dim), F32), jnp.cos(ang)], axis=0)
    sin = jnp.concatenate([jnp.zeros((n_ctx, axis_dim), F32), jnp.sin(ang)], axis=0)
    return cos.T, sin.T


def kernel(x, c, ctx, c_ctx, ada_w, ada_b, norm1_g, norm2_g, attn_w_qkv, attn_q_gain, attn_k_gain, attn_w_o, gla_w_in, gla_w_gk2_f, gla_b_gk_f, gla_w_gk2_b, gla_b_gk_b, gla_o_gain, gla_w_o, moe_w_router, moe_b_router, moe_w_gu, moe_b_gu, moe_w_down, moe_b_down):
    b, n_lat, d = x.shape
    n_ctx = ctx.shape[1]
    assert d == D_MODEL and n_ctx % _TT == 0 and n_lat % _TT == 0 and n_lat % GRID_W == 0
    nct = n_ctx // _TT
    depth = ada_w.shape[0]

    r = -(-(b + 1) // 8) * 8
    cond = jnp.zeros((r, d), F32).at[:b].set(c).at[b].set(c_ctx)
    table = _ada_table(cond, ada_w, ada_b).reshape(depth, r, N_MOD, d)
    mods = jnp.stack([jnp.broadcast_to(table[:, b:b + 1], (depth, b, N_MOD, d)), table[:, :b]],
                     axis=2)

    cos_t, sin_t = _rope_tables(n_ctx, n_lat)
    perm = jnp.concatenate([jnp.arange(0, HEAD_DIM, 2), jnp.arange(1, HEAD_DIM, 2)])
    qk_cols = (jnp.arange(N_Q_HEADS + N_KV_HEADS)[:, None] * HEAD_DIM + perm[None, :]).reshape(-1)
    cols = jnp.concatenate([qk_cols, jnp.arange(Q_DIM + KV_DIM, QKV_DIM)])

    xa = jnp.concatenate([ctx, x], axis=1)
    for i in range(depth):
        mod = mods[i]
        j = i // N_MIXERS
        g1 = norm1_g[i].reshape(1, d)
        if i % N_MIXERS == 0:
            wt = attn_w_qkv[j][:, cols].T.astype(BF16)
            gq = attn_q_gain[j][perm].reshape(HEAD_DIM, 1)
            gk = attn_k_gain[j][perm].reshape(HEAD_DIM, 1)
            qt, k, vt = _qkv_project(xa, mod, g1, wt, gq, gk, cos_t, sin_t, nct)
            xa = _attention(qt, k, vt, attn_w_o[j].astype(BF16), xa, mod, nct)
        else:
            n_main = 2 * GLA_DK + 2 * GLA_DV
            wm = gla_w_in[j][:, :n_main].astype(BF16)
            wl = jnp.zeros((d, _LANES), F32).at[:, :2 * GLA_GATE_RANK].set(
                gla_w_in[j][:, n_main:]).astype(BF16)
            w2 = jnp.zeros((_LANES, 2 * GLA_DK), F32)
            w2 = w2.at[:GLA_GATE_RANK, :GLA_DK].set(gla_w_gk2_f[j])
            w2 = w2.at[GLA_GATE_RANK:2 * GLA_GATE_RANK, GLA_DK:].set(gla_w_gk2_b[j])
            b2 = jnp.concatenate([gla_b_gk_f[j], gla_b_gk_b[j]]).reshape(1, 2 * GLA_DK)
            q, k, v, rr, gf, gb = _gla_project(xa, mod, g1, wm, wl, w2, b2, nct)
            o_f, o_b = _gla_scan(q, k, v, gf, gb, nct)
            xa = _gla_readout(o_f, o_b, rr, gla_o_gain[j].reshape(1, GLA_DV_HEAD),
                              gla_w_o[j].astype(BF16), xa, mod, nct)
        wr = jnp.zeros((d, _LANES), F32).at[:, :N_EXPERTS].set(moe_w_router[i])
        br = jnp.full((1, _LANES), _NEG, F32).at[0, :N_EXPERTS].set(moe_b_router[i])
        xa = _moe(xa, mod, norm2_g[i].reshape(1, d), wr, br,
                  i, moe_w_gu, moe_b_gu, moe_w_down, moe_b_down, nct)
    return xa[:, n_ctx:]
```

```python
import functools

import jax
import jax.numpy as jnp
from jax import lax
from jax.experimental import pallas as pl
from jax.experimental.pallas import tpu as pltpu

F32 = jnp.float32
BF16 = jnp.bfloat16

D_MODEL = 1024
DEPTH = 4
GRID_W = 64
N_MIXERS = 2
N_MOD = 6
RMS_EPS = 1e-6
HEAD_DIM = 64
N_Q_HEADS = D_MODEL // HEAD_DIM
N_KV_HEADS = N_Q_HEADS // 4
GQA_GROUP = N_Q_HEADS // N_KV_HEADS
Q_DIM = N_Q_HEADS * HEAD_DIM
KV_DIM = N_KV_HEADS * HEAD_DIM
QKV_DIM = Q_DIM + 2 * KV_DIM
V_ROWS = HEAD_DIM + 16
LOG2E = 1.4426950408889634
ROPE_THETA = 10000.0
GLA_HEADS = 4
GLA_DK = D_MODEL // 2
GLA_DV = D_MODEL
GLA_DK_HEAD = GLA_DK // GLA_HEADS
GLA_DV_HEAD = GLA_DV // GLA_HEADS
GLA_GATE_RANK = 16
GLA_GATE_NORM = 16.0
N_EXPERTS = 32
TOP_K = 4
D_EXPERT = D_MODEL
SWIGLU_LIMIT = 7.0
SWIGLU_ALPHA = 1.702

_LANES = 128
_TT = 256
_KC = 256
_HPI = 2
_GLA_SAFE_LOG_DECAY = 60.0
_TM = 512
_RUN = 8
_SLOTS = _TT * TOP_K + N_EXPERTS * _RUN
_ADA_TN = 1536
_VMEM_LIMIT = 56 * 1024 * 1024


def _cparams(*sem):
    return pltpu.CompilerParams(dimension_semantics=sem, vmem_limit_bytes=_VMEM_LIMIT)


def _dot(a, b):
    return jnp.dot(a, b, preferred_element_type=F32)


def _dot_nt(a, b):
    return lax.dot_general(a, b, (((1,), (1,)), ((), ())), preferred_element_type=F32)


def _split_bf16(a):
    hi = a.astype(BF16)
    lo = (a - hi.astype(F32)).astype(BF16)
    return hi, lo


def _dot3(a, w):
    ah, al = _split_bf16(a)
    wh, wl = _split_bf16(w)
    return _dot(ah, wh) + _dot(ah, wl) + _dot(al, wh)


def _modulate(x, g, shift, scale):
    ms = jnp.mean(x * x, axis=-1, keepdims=True)
    y = x * lax.rsqrt(ms + RMS_EPS) * g
    return y * (1.0 + scale) + shift


def _sigmoid(x):
    return 1.0 / (1.0 + jnp.exp(-x))


def _ada_kernel(cond_ref, w_ref, b_ref, o_ref):
    c = cond_ref[...]
    o_ref[0] = _dot3(c * _sigmoid(c), w_ref[0]) + b_ref[0]


def _ada_table(cond, ada_w, ada_b):
    depth, d, n = ada_w.shape
    r = cond.shape[0]
    return pl.pallas_call(
        _ada_kernel,
        grid=(depth, n // _ADA_TN),
        in_specs=[
            pl.BlockSpec((r, d), lambda i, j: (0, 0)),
            pl.BlockSpec((1, d, _ADA_TN), lambda i, j: (i, 0, j)),
            pl.BlockSpec((1, 1, _ADA_TN), lambda i, j: (i, 0, j)),
        ],
        out_specs=pl.BlockSpec((1, r, _ADA_TN), lambda i, j: (i, 0, j)),
        out_shape=jax.ShapeDtypeStruct((depth, r, n), F32),
        compiler_params=_cparams("parallel", "parallel"),
        name="ada_table",
    )(cond, ada_w, ada_b.reshape(depth, 1, n))


def _qkv_kernel(x_ref, mod_ref, g_ref, wt_ref, gq_ref, gk_ref, cos_ref, sin_ref,
                qt_ref, k_ref, vt_ref):
    mod = mod_ref[0, 0]
    h = _modulate(x_ref[0], g_ref[...], mod[0:1], mod[1:2])
    ht = h.T.astype(BF16)
    qkvt = _dot(wt_ref[...], ht)
    c = cos_ref[...]
    s = sin_ref[...]
    half = HEAD_DIM // 2

    def norm_rope(blk, gain, scale):
        ms = jnp.mean(blk * blk, axis=0, keepdims=True)
        n = blk * lax.rsqrt(ms + RMS_EPS) * gain
        x1 = n[0:half]
        x2 = n[half:HEAD_DIM]
        return jnp.concatenate([x1 * c - x2 * s, x1 * s + x2 * c], axis=0) * scale

    gq = gq_ref[...]
    gk = gk_ref[...]
    for hh in range(N_Q_HEADS):
        r0 = hh * HEAD_DIM
        qt_ref[0, r0:r0 + HEAD_DIM, :] = norm_rope(
            qkvt[r0:r0 + HEAD_DIM], gq, LOG2E * HEAD_DIM ** -0.5).astype(BF16)
    kt = jnp.concatenate(
        [norm_rope(qkvt[Q_DIM + j * HEAD_DIM:Q_DIM + (j + 1) * HEAD_DIM], gk, 1.0)
         for j in range(N_KV_HEADS)], axis=0)
    k_ref[0] = kt.T.astype(BF16)
    ones = jnp.ones((V_ROWS - HEAD_DIM, vt_ref.shape[2]), BF16)
    for j in range(N_KV_HEADS):
        v0 = Q_DIM + KV_DIM + j * HEAD_DIM
        vt_ref[0, j * V_ROWS:j * V_ROWS + HEAD_DIM, :] = qkvt[v0:v0 + HEAD_DIM].astype(BF16)
        vt_ref[0, j * V_ROWS + HEAD_DIM:(j + 1) * V_ROWS, :] = ones


def _qkv_project(xa, mod, g, wt, gq, gk, cos_t, sin_t, nct):
    b, t, d = xa.shape
    nt = t // _TT
    seg = lambda j: jnp.where(j >= nct, 1, 0)
    return pl.pallas_call(
        _qkv_kernel,
        grid=(b, nt),
        in_specs=[
            pl.BlockSpec((1, _TT, d), lambda i, j: (i, j, 0)),
            pl.BlockSpec((1, 1, N_MOD, d), lambda i, j: (i, seg(j), 0, 0)),
            pl.BlockSpec((1, d), lambda i, j: (0, 0)),
            pl.BlockSpec((QKV_DIM, d), lambda i, j: (0, 0)),
            pl.BlockSpec((HEAD_DIM, 1), lambda i, j: (0, 0)),
            pl.BlockSpec((HEAD_DIM, 1), lambda i, j: (0, 0)),
            pl.BlockSpec((HEAD_DIM // 2, _TT), lambda i, j: (0, j)),
            pl.BlockSpec((HEAD_DIM // 2, _TT), lambda i, j: (0, j)),
        ],
        out_specs=[
            pl.BlockSpec((1, Q_DIM, _TT), lambda i, j: (i, 0, j)),
            pl.BlockSpec((1, _TT, KV_DIM), lambda i, j: (i, j, 0)),
            pl.BlockSpec((1, N_KV_HEADS * V_ROWS, _TT), lambda i, j: (i, 0, j)),
        ],
        out_shape=[
            jax.ShapeDtypeStruct((b, Q_DIM, t), BF16),
            jax.ShapeDtypeStruct((b, t, KV_DIM), BF16),
            jax.ShapeDtypeStruct((b, N_KV_HEADS * V_ROWS, t), BF16),
        ],
        compiler_params=_cparams("parallel", "parallel"),
        name="attn_qkv",
    )(xa, mod, g, wt, gq, gk, cos_t, sin_t)


def _attn_kernel(qt_ref, k_ref, vt_ref, wo_ref, x_ref, mod_ref, o_ref, qe_ref, ot_ref,
                 *, nct, n_ctx):
    j = pl.program_id(1)
    t_all = k_ref.shape[1]

    def heads(nk):
        def head_pair(hp, carry):
            kvh = hp // (GQA_GROUP // _HPI)
            k0 = pl.multiple_of(kvh * HEAD_DIM, HEAD_DIM)
            v0 = pl.multiple_of(kvh * V_ROWS, 16)
            for u in range(_HPI):
                q0 = pl.multiple_of((_HPI * hp + u) * HEAD_DIM, HEAD_DIM)
                qe_ref[u] = jnp.zeros(qe_ref.shape[1:], BF16)
                qe_ref[u, pl.ds(k0, HEAD_DIM), :] = qt_ref[0, pl.ds(q0, HEAD_DIM), :]
            bounds = list(range(0, nk, _KC)) + [nk]
            chunks = list(zip(bounds[:-1], bounds[1:]))

            st = [_dot(k_ref[0, 0:nk, :], qe_ref[u]) for u in range(_HPI)]
            acc = [jnp.zeros((V_ROWS, _TT), F32) for _ in range(_HPI)]
            m_run = [jnp.full((1, _TT), -jnp.inf, F32) for _ in range(_HPI)]
            for lo, hi in chunks:
                for u in range(_HPI):
                    s_cur = st[u][lo:hi]
                    m_new = jnp.maximum(m_run[u], jnp.max(s_cur, axis=0, keepdims=True))
                    p = jnp.exp2(s_cur - m_new).astype(BF16)
                    pv = _dot(vt_ref[0, pl.ds(v0, V_ROWS), lo:hi], p)
                    acc[u] = acc[u] * jnp.exp2(m_run[u] - m_new) + pv
                    m_run[u] = m_new
            for u in range(_HPI):
                q0 = pl.multiple_of((_HPI * hp + u) * HEAD_DIM, HEAD_DIM)
                ot_ref[pl.ds(q0, HEAD_DIM), :] = acc[u][0:HEAD_DIM] / acc[u][HEAD_DIM:HEAD_DIM + 1]
            return carry

        lax.fori_loop(0, N_Q_HEADS // _HPI, head_pair, 0)

    @pl.when(j < nct)
    def _():
        heads(n_ctx)

    @pl.when(j >= nct)
    def _():
        heads(t_all)

    att = ot_ref[...].T.astype(BF16)
    out = _dot(att, wo_ref[...])
    o_ref[0] = x_ref[0] + mod_ref[0, 0][2:3] * out


def _attention(qt, k, vt, wo, xa, mod, nct):
    b, t, d = xa.shape
    nt = t // _TT
    seg = lambda j: jnp.where(j >= nct, 1, 0)
    kern = functools.partial(_attn_kernel, nct=nct, n_ctx=nct * _TT)
    return pl.pallas_call(
        kern,
        grid=(b, nt),
        in_specs=[
            pl.BlockSpec((1, Q_DIM, _TT), lambda i, j: (i, 0, j)),
            pl.BlockSpec((1, t, KV_DIM), lambda i, j: (i, 0, 0)),
            pl.BlockSpec((1, N_KV_HEADS * V_ROWS, t), lambda i, j: (i, 0, 0)),
            pl.BlockSpec((Q_DIM, d), lambda i, j: (0, 0)),
            pl.BlockSpec((1, _TT, d), lambda i, j: (i, j, 0)),
            pl.BlockSpec((1, 1, N_MOD, d), lambda i, j: (i, seg(j), 0, 0)),
        ],
        out_specs=pl.BlockSpec((1, _TT, d), lambda i, j: (i, j, 0)),
        out_shape=jax.ShapeDtypeStruct((b, t, d), F32),
        scratch_shapes=[pltpu.VMEM((_HPI, KV_DIM, _TT), BF16), pltpu.VMEM((Q_DIM, _TT), F32)],
        compiler_params=_cparams("parallel", "arbitrary"),
        name="attn_core",
    )(qt, k, vt, wo, xa, mod)


def _gla_in_kernel(x_ref, mod_ref, g_ref, wm_ref, wl_ref, w2_ref, b2_ref,
                   q_ref, k_ref, v_ref, r_ref, gf_ref, gb_ref):
    mod = mod_ref[0, 0]
    h = _modulate(x_ref[0], g_ref[...], mod[0:1], mod[1:2]).astype(BF16)
    main = _dot(h, wm_ref[...])
    q_ref[0] = (main[:, 0:GLA_DK] * GLA_DK_HEAD ** -0.5).astype(BF16)
    k_ref[0] = main[:, GLA_DK:2 * GLA_DK].astype(BF16)
    v_ref[0] = main[:, 2 * GLA_DK:2 * GLA_DK + GLA_DV].astype(BF16)
    r_ref[0] = main[:, 2 * GLA_DK + GLA_DV:].astype(BF16)
    low = _dot(h, wl_ref[...])
    pre = _dot3(low, w2_ref[...]) + b2_ref[...]
    logsig = jnp.minimum(pre, 0.0) - jnp.log(1.0 + jnp.exp(-jnp.abs(pre)))
    gate = logsig / GLA_GATE_NORM
    gf_ref[0] = gate[:, 0:GLA_DK]
    gb_ref[0] = gate[:, GLA_DK:]


def _gla_project(xa, mod, g, wm, wl, w2, b2, nct):
    b, t, d = xa.shape
    nt = t // _TT
    seg = lambda j: jnp.where(j >= nct, 1, 0)
    tile = lambda n: pl.BlockSpec((1, _TT, n), lambda i, j: (i, j, 0))
    full = lambda a: pl.BlockSpec(a.shape, lambda i, j: (0,) * a.ndim)
    return pl.pallas_call(
        _gla_in_kernel,
        grid=(b, nt),
        in_specs=[
            tile(d),
            pl.BlockSpec((1, 1, N_MOD, d), lambda i, j: (i, seg(j), 0, 0)),
            full(g), full(wm), full(wl), full(w2), full(b2),
        ],
        out_specs=[tile(GLA_DK), tile(GLA_DK), tile(GLA_DV), tile(GLA_DV),
                   tile(GLA_DK), tile(GLA_DK)],
        out_shape=[
            jax.ShapeDtypeStruct((b, t, GLA_DK), BF16),
            jax.ShapeDtypeStruct((b, t, GLA_DK), BF16),
            jax.ShapeDtypeStruct((b, t, GLA_DV), BF16),
            jax.ShapeDtypeStruct((b, t, GLA_DV), BF16),
            jax.ShapeDtypeStruct((b, t, GLA_DK), F32),
            jax.ShapeDtypeStruct((b, t, GLA_DK), F32),
        ],
        compiler_params=_cparams("parallel", "parallel"),
        name="gla_in",
    )(xa, mod, g, wm, wl, w2, b2)


def _gla_scan_kernel(qf_ref, kf_ref, vf_ref, gf_ref, qb_ref, kb_ref, vb_ref, gb_ref,
                     tril_ref, triu_ref, of_ref, ob_ref, st_ref, a_ref, kx_ref, bx_ref):
    @pl.when(pl.program_id(1) == 0)
    def _():
        st_ref[...] = jnp.zeros_like(st_ref)

    chains = []
    for h in range(GLA_HEADS):
        kcols = slice(h * GLA_DK_HEAD, (h + 1) * GLA_DK_HEAD)
        vcols = slice(h * GLA_DV_HEAD, (h + 1) * GLA_DV_HEAD)
        chains.append((qf_ref, kf_ref, vf_ref, gf_ref, tril_ref, of_ref, kcols, vcols, False))
        chains.append((qb_ref, kb_ref, vb_ref, gb_ref, triu_ref, ob_ref, kcols, vcols, True))

    def cumulative(g_ref, tri_ref):
        gh, gl = _split_bf16(g_ref[0])
        return _dot(tri_ref[...], gh) + _dot(tri_ref[...], gl)

    bcum_all = {False: cumulative(gf_ref, tril_ref), True: cumulative(gb_ref, triu_ref)}
    work = []
    for ci, (q_ref, k_ref, v_ref, g_ref, tri_ref, o_ref, kcols, vcols, reverse) in enumerate(chains):
        bcum = bcum_all[reverse][:, kcols]
        b_end = bcum[0:1] if reverse else bcum[_TT - 1:_TT]
        q = q_ref[0, :, kcols].astype(F32)
        k = k_ref[0, :, kcols].astype(F32)
        qd = (q * jnp.exp(bcum)).astype(BF16)
        kd = (k * jnp.exp(b_end - bcum)).astype(BF16)
        a_ref[ci] = _dot_nt(qd, (k * jnp.exp(-bcum)).astype(BF16))
        work.append((q, k, bcum, b_end, qd, kd))

    for ci, (q, k, bcum, b_end, qd, kd) in enumerate(work):
        @pl.when(jnp.max(-b_end) > _GLA_SAFE_LOG_DECAY)
        def _():
            kx_ref[...] = k
            bx_ref[...] = bcum
            lane = lax.broadcasted_iota(jnp.int32, (_TT, _TT), 1)

            def column(j, carry):
                decay = jnp.exp(jnp.minimum(bcum - bx_ref[pl.ds(j, 1), :], 0.0))
                col = jnp.sum(q * decay * kx_ref[pl.ds(j, 1), :], axis=1, keepdims=True)
                a_ref[ci] = jnp.where(lane == j, col, a_ref[ci])
                return carry

            lax.fori_loop(0, _TT, column, 0)

    for ci, (q_ref, k_ref, v_ref, g_ref, tri_ref, o_ref, kcols, vcols, reverse) in enumerate(chains):
        q, k, bcum, b_end, qd, kd = work[ci]
        v = v_ref[0, :, vcols]
        a = jnp.where(tri_ref[...] > 0, a_ref[ci], 0.0).astype(BF16)
        st = st_ref[ci]
        o_ref[0, :, vcols] = _dot_nt(qd, st.astype(BF16)) + _dot(a, v)
        vt = v.astype(F32).T.astype(BF16)
        st_ref[ci] = st * jnp.exp(b_end) + _dot(vt, kd)


def _gla_scan(q, k, v, gf, gb, nct):
    b, t, _ = q.shape
    nb = t // _TT

    def rev(s):
        return jnp.where(s < nct, nct - 1 - s, nb - 1 - (s - nct))

    kspec = pl.BlockSpec((1, _TT, GLA_DK), lambda i, s: (i, s, 0))
    vspec = pl.BlockSpec((1, _TT, GLA_DV), lambda i, s: (i, s, 0))
    kspec_r = pl.BlockSpec((1, _TT, GLA_DK), lambda i, s: (i, rev(s), 0))
    vspec_r = pl.BlockSpec((1, _TT, GLA_DV), lambda i, s: (i, rev(s), 0))
    tspec = pl.BlockSpec((_TT, _TT), lambda i, s: (0, 0))
    row = lax.broadcasted_iota(jnp.int32, (_TT, _TT), 0)
    col = lax.broadcasted_iota(jnp.int32, (_TT, _TT), 1)
    n_chains = 2 * GLA_HEADS
    return pl.pallas_call(
        _gla_scan_kernel,
        grid=(b, nb),
        in_specs=[kspec, kspec, vspec, kspec, kspec_r, kspec_r, vspec_r, kspec_r, tspec, tspec],
        out_specs=[vspec, vspec_r],
        out_shape=[jax.ShapeDtypeStruct((b, t, GLA_DV), F32)] * 2,
        scratch_shapes=[pltpu.VMEM((n_chains, GLA_DV_HEAD, GLA_DK_HEAD), F32),
                        pltpu.VMEM((n_chains, _TT, _TT), F32),
                        pltpu.VMEM((_TT, GLA_DK_HEAD), F32),
                        pltpu.VMEM((_TT, GLA_DK_HEAD), F32)],
        compiler_params=_cparams("parallel", "arbitrary"),
        name="gla_scan",
    )(q, k, v, gf, q, k, v, gb, (col <= row).astype(BF16), (col >= row).astype(BF16))


def _gla_out_kernel(of_ref, ob_ref, r_ref, og_ref, wo_ref, x_ref, mod_ref, o_ref):
    o = of_ref[0] + ob_ref[0]
    og = og_ref[...]
    parts = []
    for h in range(GLA_HEADS):
        oh = o[:, h * GLA_DV_HEAD:(h + 1) * GLA_DV_HEAD]
        ms = jnp.mean(oh * oh, axis=-1, keepdims=True)
        parts.append(oh * lax.rsqrt(ms + RMS_EPS) * og)
    r = r_ref[0].astype(F32)
    y = jnp.concatenate(parts, axis=1) * (r * _sigmoid(r))
    out = _dot(y.astype(BF16), wo_ref[...])
    o_ref[0] = x_ref[0] + mod_ref[0, 0][2:3] * out


def _gla_readout(o_f, o_b, r, og, wo, xa, mod, nct):
    b, t, d = xa.shape
    nt = t // _TT
    seg = lambda j: jnp.where(j >= nct, 1, 0)
    tile = lambda n: pl.BlockSpec((1, _TT, n), lambda i, j: (i, j, 0))
    return pl.pallas_call(
        _gla_out_kernel,
        grid=(b, nt),
        in_specs=[
            tile(GLA_DV), tile(GLA_DV), tile(GLA_DV),
            pl.BlockSpec((1, GLA_DV_HEAD), lambda i, j: (0, 0)),
            pl.BlockSpec((GLA_DV, d), lambda i, j: (0, 0)),
            tile(d),
            pl.BlockSpec((1, 1, N_MOD, d), lambda i, j: (i, seg(j), 0, 0)),
        ],
        out_specs=tile(d),
        out_shape=jax.ShapeDtypeStruct((b, t, d), F32),
        compiler_params=_cparams("parallel", "parallel"),
        name="gla_out",
    )(o_f, o_b, r, og, wo, xa, mod)


def _pad_run(count):
    return jnp.floor((count + (_RUN - 1)) * (1.0 / _RUN)) * _RUN


def _router_kernel(x_ref, mod_ref, g_ref, wr_ref, br_ref, h_ref, slot_ref, slotc_ref, w_ref,
                   tab_ref, tot_ref, carry_ref):
    i = pl.program_id(0)

    @pl.when(i == 0)
    def _():
        carry_ref[...] = jnp.zeros_like(carry_ref)

    mod = mod_ref[0, 0]
    h = _modulate(x_ref[...], g_ref[...], mod[3:4], mod[4:5])
    h_ref[...] = h.astype(BF16)
    logits = _dot3(h, wr_ref[...]) + br_ref[...]
    lt = logits.T[0:N_EXPERTS]
    eio = lax.broadcasted_iota(jnp.int32, lt.shape, 0)
    vals, hots = [], []
    for kk in range(TOP_K):
        m = jnp.max(lt, axis=0, keepdims=True)
        ik = jnp.min(jnp.where(lt == m, eio, N_EXPERTS), axis=0, keepdims=True)
        hot = eio == ik
        lt = jnp.where(hot, -jnp.inf, lt)
        vals.append(m)
        hots.append(hot)
    es = [jnp.exp(v - vals[0]) for v in vals]
    denom = es[0] + es[1] + es[2] + es[3]
    member = jnp.zeros(lt.shape, F32)
    for hot in hots:
        member = member + hot.astype(F32)
    member_bf = member.astype(BF16)

    r_t = lax.broadcasted_iota(jnp.int32, (_TT, _TT), 0)
    c_t = lax.broadcasted_iota(jnp.int32, (_TT, _TT), 1)
    before = (r_t < c_t).astype(BF16)
    in_run = _dot(member_bf, before)
    run_col = _pad_run(jnp.sum(member, axis=1, keepdims=True))
    r_e = lax.broadcasted_iota(jnp.int32, (N_EXPERTS, N_EXPERTS), 0)
    c_e = lax.broadcasted_iota(jnp.int32, (N_EXPERTS, N_EXPERTS), 1)
    first_col = _dot((c_e < r_e).astype(BF16),
                     jnp.broadcast_to(run_col, (N_EXPERTS, _LANES)).astype(BF16))[:, 0:1]
    slot = first_col + in_run
    rsub = lax.broadcasted_iota(jnp.int32, (_LANES, _TT), 0)
    slot_rows = jnp.zeros((_LANES, _TT), F32)
    w_rows = jnp.zeros((_LANES, _TT), F32)
    for kk in range(TOP_K):
        sk = jnp.sum(jnp.where(hots[kk], slot, 0.0), axis=0, keepdims=True)
        slot_rows = jnp.where(rsub == kk, sk, slot_rows)
        w_rows = jnp.where(rsub == kk, es[kk] / denom, w_rows)
    slot_ref[...] = slot_rows[0:8].astype(jnp.int32)
    slotc_ref[...] = slot_rows.T.astype(jnp.int32)
    w_ref[...] = w_rows.T

    member_pad = jnp.concatenate(
        [member_bf, jnp.zeros((_LANES - N_EXPERTS, _TT), BF16)], axis=0)
    run_row = _pad_run(_dot_nt(jnp.ones((8, _TT), BF16), member_pad))
    r_l = lax.broadcasted_iota(jnp.int32, (_LANES, _LANES), 0)
    c_l = lax.broadcasted_iota(jnp.int32, (_LANES, _LANES), 1)
    first_row = _dot(run_row.astype(BF16), (r_l < c_l).astype(BF16))
    earlier = carry_ref[...]
    tsub = lax.broadcasted_iota(jnp.int32, (8, _LANES), 0)
    tab = jnp.where(tsub == 0, run_row * (1.0 / _RUN),
                    jnp.where(tsub == 1, first_row, jnp.where(tsub == 2, earlier, 0.0)))
    tab_ref[0] = tab.astype(jnp.int32)
    carry_ref[...] = earlier + run_row
    tot_ref[...] = earlier + run_row


def _router(xf, mod, g, wr, br, ntb, nct):
    n, d = xf.shape
    nt = n // _TT
    seg = lambda s: jnp.where(s % ntb >= nct, 1, 0)
    return pl.pallas_call(
        _router_kernel,
        grid=(nt,),
        in_specs=[
            pl.BlockSpec((_TT, d), lambda s: (s, 0)),
            pl.BlockSpec((1, 1, N_MOD, d), lambda s: (s // ntb, seg(s), 0, 0)),
            pl.BlockSpec((1, d), lambda s: (0, 0)),
            pl.BlockSpec((d, _LANES), lambda s: (0, 0)),
            pl.BlockSpec((1, _LANES), lambda s: (0, 0)),
        ],
        out_specs=[
            pl.BlockSpec((_TT, d), lambda s: (s, 0)),
            pl.BlockSpec((8, _TT), lambda s: (0, s)),
            pl.BlockSpec((_TT, _LANES), lambda s: (s, 0)),
            pl.BlockSpec((_TT, _LANES), lambda s: (s, 0)),
            pl.BlockSpec((1, 8, _LANES), lambda s: (s, 0, 0)),
            pl.BlockSpec((8, _LANES), lambda s: (0, 0)),
        ],
        out_shape=[
            jax.ShapeDtypeStruct((n, d), BF16),
            jax.ShapeDtypeStruct((8, n), jnp.int32),
            jax.ShapeDtypeStruct((n, _LANES), jnp.int32),
            jax.ShapeDtypeStruct((n, _LANES), F32),
            jax.ShapeDtypeStruct((nt, 8, _LANES), jnp.int32),
            jax.ShapeDtypeStruct((8, _LANES), F32),
        ],
        scratch_shapes=[pltpu.VMEM((8, _LANES), F32)],
        compiler_params=_cparams("arbitrary"),
        name="moe_router",
    )(xf, mod, g, wr, br)


def _block_sizes(max_groups):
    return tuple(1 << b for b in range(max_groups.bit_length() - 1, -1, -1))


def _for_each_block(groups, sizes, fn):
    for size in sizes:
        shift = size.bit_length()
        done = ((groups >> shift) << shift) * _RUN

        @pl.when((groups & size) != 0)
        def _():
            fn(done, size * _RUN)


def _for_each_run_block(groups, fn):
    big = groups >> 3

    @pl.when(big > 0)
    def _():
        def body(i, carry):
            fn(i * (8 * _RUN), 8 * _RUN)
            return carry

        lax.fori_loop(0, big, body, 0)

    _for_each_block(groups & 7, (4, 2, 1), lambda off, rows: fn(big * (8 * _RUN) + off, rows))


def _dispatch_kernel(tab_ref, tail_ref, slot_ref, h_ref, xs_ref, xt_ref, used_ref, sem):
    s = pl.program_id(0)
    last = pl.num_programs(0) - 1
    buf = s % 2

    def wait_rows(bi, groups):
        _for_each_block(groups, _block_sizes(_SLOTS // _RUN), lambda off, rows: pltpu.make_async_copy(
            xt_ref.at[bi, pl.ds(0, rows)], xs_ref.at[pl.ds(0, rows)], sem.at[bi]).wait())

    @pl.when(s >= 2)
    def _():
        wait_rows(buf, used_ref[buf])

    slots = slot_ref[...]
    sio = lax.broadcasted_iota(jnp.int32, (_SLOTS, _TT), 0)
    perm = jnp.zeros((_SLOTS, _TT), F32)
    for kk in range(TOP_K):
        perm = perm + (sio == slots[kk:kk + 1]).astype(F32)
    xt_ref[buf] = _dot(perm.astype(BF16), h_ref[...].astype(BF16))

    def issue(e, carry):
        first = tab_ref[0, 1, e]
        base = tab_ref[0, 2, e]

        def block(off, rows):
            pltpu.make_async_copy(
                xt_ref.at[buf, pl.ds(pl.multiple_of(first + off, _RUN), rows)],
                xs_ref.at[pl.ds(pl.multiple_of(base + off, _RUN), rows)], sem.at[buf]).start()

        _for_each_run_block(tab_ref[0, 0, e], block)
        return carry

    lax.fori_loop(0, N_EXPERTS, issue, 0)
    used = tab_ref[0, 1, N_EXPERTS - 1] // _RUN + tab_ref[0, 0, N_EXPERTS - 1]
    used_ref[buf] = used

    @pl.when(s == last)
    def _():
        @pl.when(s >= 1)
        def _():
            wait_rows(1 - buf, used_ref[1 - buf])

        wait_rows(buf, used)
        xt_ref[0, 0:_TM, :] = jnp.zeros((_TM, xt_ref.shape[2]), F32)

        def tail_blocks(e, act):
            base = tail_ref[1, e]

            def block(off, rows):
                act(pltpu.make_async_copy(
                    xt_ref.at[0, pl.ds(0, rows)],
                    xs_ref.at[pl.ds(pl.multiple_of(base + off, _RUN), rows)], sem.at[0]))

            _for_each_block(tail_ref[0, e], _block_sizes(_TM // _RUN - 1), block)

        def issue_tail(e, carry):
            tail_blocks(e, lambda cp: cp.start())
            return carry

        def drain_tail(e, carry):
            tail_blocks(e, lambda cp: cp.wait())
            return carry

        lax.fori_loop(0, N_EXPERTS, issue_tail, 0)
        lax.fori_loop(0, N_EXPERTS, drain_tail, 0)

        def free_tile(i):
            row0 = pl.multiple_of(tail_ref[2, 1] + i * _TM, _TM)
            return pltpu.make_async_copy(xt_ref.at[0, pl.ds(0, _TM)], xs_ref.at[pl.ds(row0, _TM)],
                                         sem.at[0])

        def issue_free(i, carry):
            free_tile(i).start()
            return carry

        def drain_free(i, carry):
            free_tile(i).wait()
            return carry

        lax.fori_loop(0, tail_ref[2, 0], issue_free, 0)
        lax.fori_loop(0, tail_ref[2, 0], drain_free, 0)


def _dispatch(tab, tail, slot, h, rows):
    n, d = h.shape
    nt = n // _TT
    return pl.pallas_call(
        _dispatch_kernel,
        grid=(nt,),
        in_specs=[
            pl.BlockSpec((1, 8, _LANES), lambda s: (s, 0, 0), memory_space=pltpu.SMEM),
            pl.BlockSpec((8, _LANES), lambda s: (0, 0), memory_space=pltpu.SMEM),
            pl.BlockSpec((8, _TT), lambda s: (0, s)),
            pl.BlockSpec((_TT, d), lambda s: (s, 0)),
        ],
        out_specs=pl.BlockSpec(memory_space=pl.ANY),
        out_shape=jax.ShapeDtypeStruct((rows, d), F32),
        scratch_shapes=[pltpu.VMEM((2, _SLOTS, d), F32), pltpu.SMEM((2,), jnp.int32),
                        pltpu.SemaphoreType.DMA((2,))],
        compiler_params=_cparams("arbitrary"),
        name="moe_dispatch",
    )(tab, tail, slot, h)


def _expert_kernel(te_ref, nu_ref, xs_ref, wgu_ref, bgu_ref, wd_ref, bd_ref, ys_ref,
                   wgu_bf_ref, wd_bf_ref):
    t = pl.program_id(0)
    live = t < nu_ref[0]
    new_expert = jnp.logical_or(t == 0, te_ref[t] != te_ref[jnp.maximum(t - 1, 0)])

    @pl.when(jnp.logical_and(live, new_expert))
    def _():
        wgu_bf_ref[...] = wgu_ref[0, 0].astype(BF16)
        wd_bf_ref[...] = wd_ref[0, 0].astype(BF16)

    @pl.when(live)
    def _():
        x = xs_ref[...].astype(BF16)
        gu = _dot(x, wgu_bf_ref[...]) + bgu_ref[0, 0]
        gate = jnp.minimum(gu[:, 0:D_EXPERT], SWIGLU_LIMIT)
        up = jnp.clip(gu[:, D_EXPERT:], -SWIGLU_LIMIT, SWIGLU_LIMIT)
        act = (up + 1.0) * (gate * _sigmoid(SWIGLU_ALPHA * gate))
        ys_ref[...] = _dot(act.astype(BF16), wd_bf_ref[...]) + bd_ref[0, 0]

    @pl.when(jnp.logical_not(live))
    def _():
        ys_ref[...] = jnp.zeros_like(ys_ref)


def _experts(tile_expert, n_used, xs, layer, wgu, bgu, wd, bd):
    p, d = xs.shape
    nt = p // _TM
    depth, e, _, n2 = wgu.shape
    grid_spec = pltpu.PrefetchScalarGridSpec(
        num_scalar_prefetch=2,
        grid=(nt,),
        in_specs=[
            pl.BlockSpec((_TM, d), lambda t, te, nu: (jnp.minimum(t, nu[0] - 1), 0)),
            pl.BlockSpec((1, 1, d, n2), lambda t, te, nu: (layer, te[t], 0, 0)),
            pl.BlockSpec((1, 1, 1, n2), lambda t, te, nu: (layer, te[t], 0, 0)),
            pl.BlockSpec((1, 1, D_EXPERT, d), lambda t, te, nu: (layer, te[t], 0, 0)),
            pl.BlockSpec((1, 1, 1, d), lambda t, te, nu: (layer, te[t], 0, 0)),
        ],
        out_specs=pl.BlockSpec((_TM, d), lambda t, te, nu: (t, 0)),
        scratch_shapes=[pltpu.VMEM((d, n2), BF16), pltpu.VMEM((D_EXPERT, d), BF16)],
    )
    return pl.pallas_call(
        _expert_kernel,
        grid_spec=grid_spec,
        out_shape=jax.ShapeDtypeStruct((p, d), F32),
        compiler_params=_cparams("arbitrary"),
        name="moe_experts",
    )(tile_expert, n_used, xs, wgu, bgu.reshape(depth, e, 1, n2), wd, bd.reshape(depth, e, 1, d))


def _combine_kernel(tab_ref, tabn_ref, slotc_ref, w_ref, ys_ref, x_ref, mod_ref, o_ref, yb_ref, sem):
    s = pl.program_id(0)
    buf = s % 2

    def issue_runs(t_ref, bi):
        def issue(e, carry):
            first = t_ref[0, 1, e]
            base = t_ref[0, 2, e]

            def block(off, rows):
                pltpu.make_async_copy(
                    ys_ref.at[pl.ds(pl.multiple_of(base + off, _RUN), rows)],
                    yb_ref.at[bi, pl.ds(pl.multiple_of(first + off, _RUN), rows)], sem.at[bi]).start()

            _for_each_run_block(t_ref[0, 0, e], block)
            return carry

        lax.fori_loop(0, N_EXPERTS, issue, 0)

    @pl.when(s == 0)
    def _():
        yb_ref[...] = jnp.zeros_like(yb_ref)
        issue_runs(tab_ref, 0)

    @pl.when(s + 1 < pl.num_programs(0))
    def _():
        issue_runs(tabn_ref, 1 - buf)

    used = tab_ref[0, 1, N_EXPERTS - 1] + tab_ref[0, 0, N_EXPERTS - 1] * _RUN
    _for_each_block(used // _RUN, _block_sizes(_SLOTS // _RUN), lambda off, rows: pltpu.make_async_copy(
        ys_ref.at[pl.ds(0, rows)], yb_ref.at[buf, pl.ds(0, rows)], sem.at[buf]).wait())
    rio = lax.broadcasted_iota(jnp.int32, (_SLOTS, 1), 0)
    y = jnp.where(rio < used, yb_ref[buf], 0.0).astype(BF16)
    lio = lax.broadcasted_iota(jnp.int32, (_TT, _SLOTS), 1)
    slotc = slotc_ref[...]
    w = w_ref[...]
    gates = jnp.zeros((_TT, _SLOTS), F32)
    for kk in range(TOP_K):
        gates = gates + jnp.where(lio == slotc[:, kk:kk + 1], w[:, kk:kk + 1], 0.0)
    f = _dot(gates.astype(BF16), y)
    o_ref[...] = x_ref[...] + mod_ref[0, 0][5:6] * f


def _combine(tab, slotc, w, ys, xf, mod, ntb, nct):
    n, d = xf.shape
    nt = n // _TT
    seg = lambda s: jnp.where(s % ntb >= nct, 1, 0)
    return pl.pallas_call(
        _combine_kernel,
        grid=(nt,),
        in_specs=[
            pl.BlockSpec((1, 8, _LANES), lambda s: (s, 0, 0), memory_space=pltpu.SMEM),
            pl.BlockSpec((1, 8, _LANES), lambda s: (jnp.minimum(s + 1, nt - 1), 0, 0),
                         memory_space=pltpu.SMEM),
            pl.BlockSpec((_TT, _LANES), lambda s: (s, 0)),
            pl.BlockSpec((_TT, _LANES), lambda s: (s, 0)),
            pl.BlockSpec(memory_space=pl.ANY),
            pl.BlockSpec((_TT, d), lambda s: (s, 0)),
            pl.BlockSpec((1, 1, N_MOD, d), lambda s: (s // ntb, seg(s), 0, 0)),
        ],
        out_specs=pl.BlockSpec((_TT, d), lambda s: (s, 0)),
        out_shape=jax.ShapeDtypeStruct((n, d), F32),
        scratch_shapes=[pltpu.VMEM((2, _SLOTS, d), F32), pltpu.SemaphoreType.DMA((2,))],
        compiler_params=_cparams("arbitrary"),
        name="moe_combine",
    )(tab, tab, slotc, w, ys, xf, mod)


def _moe(xa, mod, g2, wr, br, layer, wgu, bgu, wd, bd, nct):
    b, t, d = xa.shape
    n = b * t
    ntb = t // _TT
    xf = xa.reshape(n, d)
    nt = n // _TT
    h, slot, slotc, w, tab, tot = _router(xf, mod, g2, wr, br, ntb, nct)
    run_rows = tot[0].astype(jnp.int32)
    padded = (run_rows + _TM - 1) // _TM * _TM
    ends = jnp.cumsum(padded)
    starts = ends - padded
    row = lax.broadcasted_iota(jnp.int32, (8, _LANES), 0)
    tab = tab + jnp.where(row == 2, starts[None, :], 0)[None]
    max_rows = n * TOP_K + nt * N_EXPERTS * (_RUN - 1) + N_EXPERTS * (_TM - 1)
    n_tiles = max_rows // _TM
    lane = lax.broadcasted_iota(jnp.int32, (8, _LANES), 1)
    used_rows = ends[N_EXPERTS - 1]
    free = jnp.where(lane == 0, n_tiles - used_rows // _TM, jnp.where(lane == 1, used_rows, 0))
    tail = jnp.where(row == 0, ((padded - run_rows) // _RUN)[None, :],
                     jnp.where(row == 1, (starts + run_rows)[None, :],
                               jnp.where(row == 2, free, 0)))
    tile_row = jnp.arange(n_tiles, dtype=jnp.int32) * _TM
    tile_expert = jnp.minimum(jnp.sum(tile_row[:, None] >= ends[None, :N_EXPERTS], axis=1),
                              N_EXPERTS - 1).astype(jnp.int32)
    n_used = (ends[N_EXPERTS - 1:N_EXPERTS] // _TM).astype(jnp.int32)
    xs = _dispatch(tab, tail, slot, h, n_tiles * _TM)
    ys = _experts(tile_expert, n_used, xs, layer, wgu, bgu, wd, bd)
    return _combine(tab, slotc, w, ys, xf, mod, ntb, nct).reshape(b, t, d)


def _rope_tables(n_ctx, n_lat):
    rows = n_lat // GRID_W
    row_ids = jnp.repeat(jnp.arange(rows), GRID_W).astype(F32)
    col_ids = jnp.tile(jnp.arange(GRID_W), rows).astype(F32)
    axis_dim = HEAD_DIM // 2
    inv_freq = 1.0 / (ROPE_THETA ** (jnp.arange(0, axis_dim, 2, dtype=F32) / axis_dim))
    ang = jnp.concatenate([row_ids[:, None] * inv_freq, col_ids[:, None] * inv_freq], axis=-1)
    cos = jnp.concatenate([jnp.ones((n_ctx, axis_dim), F32), jnp.cos(ang)], axis=0)
    sin = jnp.concatenate([jnp.zeros((n_ctx, axis_dim), F32), jnp.sin(ang)], axis=0)
    return cos.T, sin.T


def kernel(x, c, ctx, c_ctx, ada_w, ada_b, norm1_g, norm2_g, attn_w_qkv, attn_q_gain, attn_k_gain, attn_w_o, gla_w_in, gla_w_gk2_f, gla_b_gk_f, gla_w_gk2_b, gla_b_gk_b, gla_o_gain, gla_w_o, moe_w_router, moe_b_router, moe_w_gu, moe_b_gu, moe_w_down, moe_b_down):
    b, n_lat, d = x.shape
    n_ctx = ctx.shape[1]
    assert d == D_MODEL and n_ctx % _TT == 0 and n_lat % _TT == 0 and n_lat % GRID_W == 0
    nct = n_ctx // _TT
    depth = ada_w.shape[0]

    r = -(-(b + 1) // 8) * 8
    cond = jnp.zeros((r, d), F32).at[:b].set(c).at[b].set(c_ctx)
    table = _ada_table(cond, ada_w, ada_b).reshape(depth, r, N_MOD, d)
    mods = jnp.stack([jnp.broadcast_to(table[:, b:b + 1], (depth, b, N_MOD, d)), table[:, :b]],
                     axis=2)

    cos_t, sin_t = _rope_tables(n_ctx, n_lat)
    perm = jnp.concatenate([jnp.arange(0, HEAD_DIM, 2), jnp.arange(1, HEAD_DIM, 2)])
    qk_cols = (jnp.arange(N_Q_HEADS + N_KV_HEADS)[:, None] * HEAD_DIM + perm[None, :]).reshape(-1)
    cols = jnp.concatenate([qk_cols, jnp.arange(Q_DIM + KV_DIM, QKV_DIM)])

    xa = jnp.concatenate([ctx, x], axis=1)
    for i in range(depth):
        mod = mods[i]
        j = i // N_MIXERS
        g1 = norm1_g[i].reshape(1, d)
        if i % N_MIXERS == 0:
            wt = attn_w_qkv[j][:, cols].T.astype(BF16)
            gq = attn_q_gain[j][perm].reshape(HEAD_DIM, 1)
            gk = attn_k_gain[j][perm].reshape(HEAD_DIM, 1)
            qt, k, vt = _qkv_project(xa, mod, g1, wt, gq, gk, cos_t, sin_t, nct)
            xa = _attention(qt, k, vt, attn_w_o[j].astype(BF16), xa, mod, nct)
        else:
            n_main = 2 * GLA_DK + 2 * GLA_DV
            wm = gla_w_in[j][:, :n_main].astype(BF16)
            wl = jnp.zeros((d, _LANES), F32).at[:, :2 * GLA_GATE_RANK].set(
                gla_w_in[j][:, n_main:]).astype(BF16)
            w2 = jnp.zeros((_LANES, 2 * GLA_DK), F32)
            w2 = w2.at[:GLA_GATE_RANK, :GLA_DK].set(gla_w_gk2_f[j])
            w2 = w2.at[GLA_GATE_RANK:2 * GLA_GATE_RANK, GLA_DK:].set(gla_w_gk2_b[j])
            b2 = jnp.concatenate([gla_b_gk_f[j], gla_b_gk_b[j]]).reshape(1, 2 * GLA_DK)
            q, k, v, rr, gf, gb = _gla_project(xa, mod, g1, wm, wl, w2, b2, nct)
            o_f, o_b = _gla_scan(q, k, v, gf, gb, nct)
            xa = _gla_readout(o_f, o_b, rr, gla_o_gain[j].reshape(1, GLA_DV_HEAD),
                              gla_w_o[j].astype(BF16), xa, mod, nct)
        wr = jnp.zeros((d, _LANES), F32).at[:, :N_EXPERTS].set(moe_w_router[i])
        br = jnp.zeros((1, _LANES), F32).at[0, :N_EXPERTS].set(moe_b_router[i])
        if i == depth - 1:
            return _moe(xa[:, n_ctx:], mod, norm2_g[i].reshape(1, d), wr, br,
                        i, moe_w_gu, moe_b_gu, moe_w_down, moe_b_down, 0)
        xa = _moe(xa, mod, norm2_g[i].reshape(1, d), wr, br,
                  i, moe_w_gu, moe_b_gu, moe_w_down, moe_b_down, nct)
    return xa[:, n_ctx:]
```

```python
import functools

import jax
import jax.numpy as jnp
from jax import lax
from jax.experimental import pallas as pl
from jax.experimental.pallas import tpu as pltpu

F32 = jnp.float32
BF16 = jnp.bfloat16

D_MODEL = 1024
DEPTH = 4
GRID_W = 64
N_MIXERS = 2
N_MOD = 6
RMS_EPS = 1e-6
HEAD_DIM = 64
N_Q_HEADS = D_MODEL // HEAD_DIM
N_KV_HEADS = N_Q_HEADS // 4
GQA_GROUP = N_Q_HEADS // N_KV_HEADS
Q_DIM = N_Q_HEADS * HEAD_DIM
KV_DIM = N_KV_HEADS * HEAD_DIM
QKV_DIM = Q_DIM + 2 * KV_DIM
V_ROWS = HEAD_DIM + 16
LOG2E = 1.4426950408889634
ROPE_THETA = 10000.0
GLA_HEADS = 4
GLA_DK = D_MODEL // 2
GLA_DV = D_MODEL
GLA_DK_HEAD = GLA_DK // GLA_HEADS
GLA_DV_HEAD = GLA_DV // GLA_HEADS
GLA_GATE_RANK = 16
GLA_GATE_NORM = 16.0
N_EXPERTS = 32
TOP_K = 4
D_EXPERT = D_MODEL
SWIGLU_LIMIT = 7.0
SWIGLU_ALPHA = 1.702

_LANES = 128
_TT = 256
_KC = 256
_HPI = 2
_GLA_SAFE_LOG_DECAY = 60.0
_TM = 512
_RUN = 8
_SLOTS = _TT * TOP_K + N_EXPERTS * _RUN
_ADA_TN = 1536
_VMEM_LIMIT = 56 * 1024 * 1024


def _cparams(*sem):
    return pltpu.CompilerParams(dimension_semantics=sem, vmem_limit_bytes=_VMEM_LIMIT)


def _dot(a, b):
    return jnp.dot(a, b, preferred_element_type=F32)


def _dot_nt(a, b):
    return lax.dot_general(a, b, (((1,), (1,)), ((), ())), preferred_element_type=F32)


def _split_bf16(a):
    hi = a.astype(BF16)
    lo = (a - hi.astype(F32)).astype(BF16)
    return hi, lo


def _dot3(a, w):
    ah, al = _split_bf16(a)
    wh, wl = _split_bf16(w)
    return _dot(ah, wh) + _dot(ah, wl) + _dot(al, wh)


def _modulate(x, g, shift, scale):
    ms = jnp.mean(x * x, axis=-1, keepdims=True)
    y = x * lax.rsqrt(ms + RMS_EPS) * g
    return y * (1.0 + scale) + shift


def _sigmoid(x):
    return 1.0 / (1.0 + jnp.exp(-x))


def _ada_kernel(cond_ref, w_ref, b_ref, o_ref):
    c = cond_ref[...]
    o_ref[0] = _dot3(c * _sigmoid(c), w_ref[0]) + b_ref[0]


def _ada_table(cond, ada_w, ada_b):
    depth, d, n = ada_w.shape
    r = cond.shape[0]
    return pl.pallas_call(
        _ada_kernel,
        grid=(depth, n // _ADA_TN),
        in_specs=[
            pl.BlockSpec((r, d), lambda i, j: (0, 0)),
            pl.BlockSpec((1, d, _ADA_TN), lambda i, j: (i, 0, j)),
            pl.BlockSpec((1, 1, _ADA_TN), lambda i, j: (i, 0, j)),
        ],
        out_specs=pl.BlockSpec((1, r, _ADA_TN), lambda i, j: (i, 0, j)),
        out_shape=jax.ShapeDtypeStruct((depth, r, n), F32),
        compiler_params=_cparams("parallel", "parallel"),
        name="ada_table",
    )(cond, ada_w, ada_b.reshape(depth, 1, n))


def _qkv_kernel(x_ref, mod_ref, g_ref, wt_ref, gq_ref, gk_ref, cos_ref, sin_ref,
                qt_ref, k_ref, vt_ref):
    mod = mod_ref[0, 0]
    h = _modulate(x_ref[0], g_ref[...], mod[0:1], mod[1:2])
    ht = h.T.astype(BF16)
    qkvt = _dot(wt_ref[...], ht)
    c = cos_ref[...]
    s = sin_ref[...]
    half = HEAD_DIM // 2

    def norm_rope(blk, gain, scale):
        ms = jnp.mean(blk * blk, axis=0, keepdims=True)
        n = blk * lax.rsqrt(ms + RMS_EPS) * gain
        x1 = n[0:half]
        x2 = n[half:HEAD_DIM]
        return jnp.concatenate([x1 * c - x2 * s, x1 * s + x2 * c], axis=0) * scale

    gq = gq_ref[...]
    gk = gk_ref[...]
    for hh in range(N_Q_HEADS):
        r0 = hh * HEAD_DIM
        qt_ref[0, r0:r0 + HEAD_DIM, :] = norm_rope(
            qkvt[r0:r0 + HEAD_DIM], gq, LOG2E * HEAD_DIM ** -0.5).astype(BF16)
    kt = jnp.concatenate(
        [norm_rope(qkvt[Q_DIM + j * HEAD_DIM:Q_DIM + (j + 1) * HEAD_DIM], gk, 1.0)
         for j in range(N_KV_HEADS)], axis=0)
    k_ref[0] = kt.T.astype(BF16)
    ones = jnp.ones((V_ROWS - HEAD_DIM, vt_ref.shape[2]), BF16)
    for j in range(N_KV_HEADS):
        v0 = Q_DIM + KV_DIM + j * HEAD_DIM
        vt_ref[0, j * V_ROWS:j * V_ROWS + HEAD_DIM, :] = qkvt[v0:v0 + HEAD_DIM].astype(BF16)
        vt_ref[0, j * V_ROWS + HEAD_DIM:(j + 1) * V_ROWS, :] = ones


def _qkv_project(xa, mod, g, wt, gq, gk, cos_t, sin_t, nct):
    b, t, d = xa.shape
    nt = t // _TT
    seg = lambda j: jnp.where(j >= nct, 1, 0)
    return pl.pallas_call(
        _qkv_kernel,
        grid=(b, nt),
        in_specs=[
            pl.BlockSpec((1, _TT, d), lambda i, j: (i, j, 0)),
            pl.BlockSpec((1, 1, N_MOD, d), lambda i, j: (i, seg(j), 0, 0)),
            pl.BlockSpec((1, d), lambda i, j: (0, 0)),
            pl.BlockSpec((QKV_DIM, d), lambda i, j: (0, 0)),
            pl.BlockSpec((HEAD_DIM, 1), lambda i, j: (0, 0)),
            pl.BlockSpec((HEAD_DIM, 1), lambda i, j: (0, 0)),
            pl.BlockSpec((HEAD_DIM // 2, _TT), lambda i, j: (0, j)),
            pl.BlockSpec((HEAD_DIM // 2, _TT), lambda i, j: (0, j)),
        ],
        out_specs=[
            pl.BlockSpec((1, Q_DIM, _TT), lambda i, j: (i, 0, j)),
            pl.BlockSpec((1, _TT, KV_DIM), lambda i, j: (i, j, 0)),
            pl.BlockSpec((1, N_KV_HEADS * V_ROWS, _TT), lambda i, j: (i, 0, j)),
        ],
        out_shape=[
            jax.ShapeDtypeStruct((b, Q_DIM, t), BF16),
            jax.ShapeDtypeStruct((b, t, KV_DIM), BF16),
            jax.ShapeDtypeStruct((b, N_KV_HEADS * V_ROWS, t), BF16),
        ],
        compiler_params=_cparams("parallel", "parallel"),
        name="attn_qkv",
    )(xa, mod, g, wt, gq, gk, cos_t, sin_t)


def _attn_kernel(qt_ref, k_ref, vt_ref, wo_ref, x_ref, mod_ref, o_ref, qe_ref, ot_ref,
                 *, nct, n_ctx):
    j = pl.program_id(1)
    t_all = k_ref.shape[1]

    def heads(nk):
        def head_pair(hp, carry):
            kvh = hp // (GQA_GROUP // _HPI)
            k0 = pl.multiple_of(kvh * HEAD_DIM, HEAD_DIM)
            v0 = pl.multiple_of(kvh * V_ROWS, 16)
            for u in range(_HPI):
                q0 = pl.multiple_of((_HPI * hp + u) * HEAD_DIM, HEAD_DIM)
                qe_ref[u] = jnp.zeros(qe_ref.shape[1:], BF16)
                qe_ref[u, pl.ds(k0, HEAD_DIM), :] = qt_ref[0, pl.ds(q0, HEAD_DIM), :]
            bounds = list(range(0, nk, _KC)) + [nk]
            chunks = list(zip(bounds[:-1], bounds[1:]))

            st = [_dot(k_ref[0, 0:nk, :], qe_ref[u]) for u in range(_HPI)]
            acc = [jnp.zeros((V_ROWS, _TT), F32) for _ in range(_HPI)]
            m_run = [jnp.full((1, _TT), -jnp.inf, F32) for _ in range(_HPI)]
            for lo, hi in chunks:
                for u in range(_HPI):
                    s_cur = st[u][lo:hi]
                    m_new = jnp.maximum(m_run[u], jnp.max(s_cur, axis=0, keepdims=True))
                    p = jnp.exp2(s_cur - m_new).astype(BF16)
                    pv = _dot(vt_ref[0, pl.ds(v0, V_ROWS), lo:hi], p)
                    acc[u] = acc[u] * jnp.exp2(m_run[u] - m_new) + pv
                    m_run[u] = m_new
            for u in range(_HPI):
                q0 = pl.multiple_of((_HPI * hp + u) * HEAD_DIM, HEAD_DIM)
                ot_ref[pl.ds(q0, HEAD_DIM), :] = acc[u][0:HEAD_DIM] / acc[u][HEAD_DIM:HEAD_DIM + 1]
            return carry

        lax.fori_loop(0, N_Q_HEADS // _HPI, head_pair, 0)

    @pl.when(j < nct)
    def _():
        heads(n_ctx)

    @pl.when(j >= nct)
    def _():
        heads(t_all)

    att = ot_ref[...].T.astype(BF16)
    out = _dot(att, wo_ref[...])
    o_ref[0] = x_ref[0] + mod_ref[0, 0][2:3] * out


def _attention(qt, k, vt, wo, xa, mod, nct):
    b, t, d = xa.shape
    nt = t // _TT
    seg = lambda j: jnp.where(j >= nct, 1, 0)
    kern = functools.partial(_attn_kernel, nct=nct, n_ctx=nct * _TT)
    return pl.pallas_call(
        kern,
        grid=(b, nt),
        in_specs=[
            pl.BlockSpec((1, Q_DIM, _TT), lambda i, j: (i, 0, j)),
            pl.BlockSpec((1, t, KV_DIM), lambda i, j: (i, 0, 0)),
            pl.BlockSpec((1, N_KV_HEADS * V_ROWS, t), lambda i, j: (i, 0, 0)),
            pl.BlockSpec((Q_DIM, d), lambda i, j: (0, 0)),
            pl.BlockSpec((1, _TT, d), lambda i, j: (i, j, 0)),
            pl.BlockSpec((1, 1, N_MOD, d), lambda i, j: (i, seg(j), 0, 0)),
        ],
        out_specs=pl.BlockSpec((1, _TT, d), lambda i, j: (i, j, 0)),
        out_shape=jax.ShapeDtypeStruct((b, t, d), F32),
        scratch_shapes=[pltpu.VMEM((_HPI, KV_DIM, _TT), BF16), pltpu.VMEM((Q_DIM, _TT), F32)],
        compiler_params=_cparams("parallel", "arbitrary"),
        name="attn_core",
    )(qt, k, vt, wo, xa, mod)


def _gla_in_kernel(x_ref, mod_ref, g_ref, wm_ref, wl_ref, w2_ref, b2_ref,
                   q_ref, k_ref, v_ref, r_ref, gf_ref, gb_ref):
    mod = mod_ref[0, 0]
    h = _modulate(x_ref[0], g_ref[...], mod[0:1], mod[1:2]).astype(BF16)
    main = _dot(h, wm_ref[...])
    q_ref[0] = (main[:, 0:GLA_DK] * GLA_DK_HEAD ** -0.5).astype(BF16)
    k_ref[0] = main[:, GLA_DK:2 * GLA_DK].astype(BF16)
    v_ref[0] = main[:, 2 * GLA_DK:2 * GLA_DK + GLA_DV].astype(BF16)
    r_ref[0] = main[:, 2 * GLA_DK + GLA_DV:].astype(BF16)
    low = _dot(h, wl_ref[...])
    pre = _dot3(low, w2_ref[...]) + b2_ref[...]
    logsig = jnp.minimum(pre, 0.0) - jnp.log(1.0 + jnp.exp(-jnp.abs(pre)))
    gate = logsig / GLA_GATE_NORM
    gf_ref[0] = gate[:, 0:GLA_DK]
    gb_ref[0] = gate[:, GLA_DK:]


def _gla_project(xa, mod, g, wm, wl, w2, b2, nct):
    b, t, d = xa.shape
    nt = t // _TT
    seg = lambda j: jnp.where(j >= nct, 1, 0)
    tile = lambda n: pl.BlockSpec((1, _TT, n), lambda i, j: (i, j, 0))
    full = lambda a: pl.BlockSpec(a.shape, lambda i, j: (0,) * a.ndim)
    return pl.pallas_call(
        _gla_in_kernel,
        grid=(b, nt),
        in_specs=[
            tile(d),
            pl.BlockSpec((1, 1, N_MOD, d), lambda i, j: (i, seg(j), 0, 0)),
            full(g), full(wm), full(wl), full(w2), full(b2),
        ],
        out_specs=[tile(GLA_DK), tile(GLA_DK), tile(GLA_DV), tile(GLA_DV),
                   tile(GLA_DK), tile(GLA_DK)],
        out_shape=[
            jax.ShapeDtypeStruct((b, t, GLA_DK), BF16),
            jax.ShapeDtypeStruct((b, t, GLA_DK), BF16),
            jax.ShapeDtypeStruct((b, t, GLA_DV), BF16),
            jax.ShapeDtypeStruct((b, t, GLA_DV), BF16),
            jax.ShapeDtypeStruct((b, t, GLA_DK), F32),
            jax.ShapeDtypeStruct((b, t, GLA_DK), F32),
        ],
        compiler_params=_cparams("parallel", "parallel"),
        name="gla_in",
    )(xa, mod, g, wm, wl, w2, b2)


def _gla_scan_kernel(qf_ref, kf_ref, vf_ref, gf_ref, qb_ref, kb_ref, vb_ref, gb_ref,
                     tril_ref, triu_ref, of_ref, ob_ref, st_ref, a_ref, kx_ref, bx_ref):
    @pl.when(pl.program_id(1) == 0)
    def _():
        st_ref[...] = jnp.zeros_like(st_ref)

    chains = []
    for h in range(GLA_HEADS):
        kcols = slice(h * GLA_DK_HEAD, (h + 1) * GLA_DK_HEAD)
        vcols = slice(h * GLA_DV_HEAD, (h + 1) * GLA_DV_HEAD)
        chains.append((qf_ref, kf_ref, vf_ref, gf_ref, tril_ref, of_ref, kcols, vcols, False))
        chains.append((qb_ref, kb_ref, vb_ref, gb_ref, triu_ref, ob_ref, kcols, vcols, True))

    def cumulative(g_ref, tri_ref):
        gh, gl = _split_bf16(g_ref[0])
        return _dot(tri_ref[...], gh) + _dot(tri_ref[...], gl)

    bcum_all = {False: cumulative(gf_ref, tril_ref), True: cumulative(gb_ref, triu_ref)}
    work = []
    for ci, (q_ref, k_ref, v_ref, g_ref, tri_ref, o_ref, kcols, vcols, reverse) in enumerate(chains):
        bcum = bcum_all[reverse][:, kcols]
        b_end = bcum[0:1] if reverse else bcum[_TT - 1:_TT]
        q = q_ref[0, :, kcols].astype(F32)
        k = k_ref[0, :, kcols].astype(F32)
        qd = (q * jnp.exp(bcum)).astype(BF16)
        kd = (k * jnp.exp(b_end - bcum)).astype(BF16)
        a_ref[ci] = _dot_nt(qd, (k * jnp.exp(-bcum)).astype(BF16))
        work.append((q, k, bcum, b_end, qd, kd))

    for ci, (q, k, bcum, b_end, qd, kd) in enumerate(work):
        @pl.when(jnp.max(-b_end) > _GLA_SAFE_LOG_DECAY)
        def _():
            kx_ref[...] = k
            bx_ref[...] = bcum
            lane = lax.broadcasted_iota(jnp.int32, (_TT, _TT), 1)

            def column(j, carry):
                decay = jnp.exp(jnp.minimum(bcum - bx_ref[pl.ds(j, 1), :], 0.0))
                col = jnp.sum(q * decay * kx_ref[pl.ds(j, 1), :], axis=1, keepdims=True)
                a_ref[ci] = jnp.where(lane == j, col, a_ref[ci])
                return carry

            lax.fori_loop(0, _TT, column, 0)

    for ci, (q_ref, k_ref, v_ref, g_ref, tri_ref, o_ref, kcols, vcols, reverse) in enumerate(chains):
        q, k, bcum, b_end, qd, kd = work[ci]
        v = v_ref[0, :, vcols]
        a = jnp.where(tri_ref[...] > 0, a_ref[ci], 0.0).astype(BF16)
        st = st_ref[ci]
        o_ref[0, :, vcols] = (_dot_nt(qd, st.astype(BF16)) + _dot(a, v)).astype(BF16)
        vt = v.astype(F32).T.astype(BF16)
        st_ref[ci] = st * jnp.exp(b_end) + _dot(vt, kd)


def _gla_scan(q, k, v, gf, gb, nct):
    b, t, _ = q.shape
    nb = t // _TT

    def rev(s):
        return jnp.where(s < nct, nct - 1 - s, nb - 1 - (s - nct))

    kspec = pl.BlockSpec((1, _TT, GLA_DK), lambda i, s: (i, s, 0))
    vspec = pl.BlockSpec((1, _TT, GLA_DV), lambda i, s: (i, s, 0))
    kspec_r = pl.BlockSpec((1, _TT, GLA_DK), lambda i, s: (i, rev(s), 0))
    vspec_r = pl.BlockSpec((1, _TT, GLA_DV), lambda i, s: (i, rev(s), 0))
    tspec = pl.BlockSpec((_TT, _TT), lambda i, s: (0, 0))
    row = lax.broadcasted_iota(jnp.int32, (_TT, _TT), 0)
    col = lax.broadcasted_iota(jnp.int32, (_TT, _TT), 1)
    n_chains = 2 * GLA_HEADS
    return pl.pallas_call(
        _gla_scan_kernel,
        grid=(b, nb),
        in_specs=[kspec, kspec, vspec, kspec, kspec_r, kspec_r, vspec_r, kspec_r, tspec, tspec],
        out_specs=[vspec, vspec_r],
        out_shape=[jax.ShapeDtypeStruct((b, t, GLA_DV), BF16)] * 2,
        scratch_shapes=[pltpu.VMEM((n_chains, GLA_DV_HEAD, GLA_DK_HEAD), F32),
                        pltpu.VMEM((n_chains, _TT, _TT), F32),
                        pltpu.VMEM((_TT, GLA_DK_HEAD), F32),
                        pltpu.VMEM((_TT, GLA_DK_HEAD), F32)],
        compiler_params=_cparams("parallel", "arbitrary"),
        name="gla_scan",
    )(q, k, v, gf, q, k, v, gb, (col <= row).astype(BF16), (col >= row).astype(BF16))


def _gla_out_kernel(of_ref, ob_ref, r_ref, og_ref, wo_ref, x_ref, mod_ref, o_ref):
    o = of_ref[0].astype(F32) + ob_ref[0].astype(F32)
    og = og_ref[...]
    parts = []
    for h in range(GLA_HEADS):
        oh = o[:, h * GLA_DV_HEAD:(h + 1) * GLA_DV_HEAD]
        ms = jnp.mean(oh * oh, axis=-1, keepdims=True)
        parts.append(oh * lax.rsqrt(ms + RMS_EPS) * og)
    r = r_ref[0].astype(F32)
    y = jnp.concatenate(parts, axis=1) * (r * _sigmoid(r))
    out = _dot(y.astype(BF16), wo_ref[...])
    o_ref[0] = x_ref[0] + mod_ref[0, 0][2:3] * out


def _gla_readout(o_f, o_b, r, og, wo, xa, mod, nct):
    b, t, d = xa.shape
    nt = t // _TT
    seg = lambda j: jnp.where(j >= nct, 1, 0)
    tile = lambda n: pl.BlockSpec((1, _TT, n), lambda i, j: (i, j, 0))
    return pl.pallas_call(
        _gla_out_kernel,
        grid=(b, nt),
        in_specs=[
            tile(GLA_DV), tile(GLA_DV), tile(GLA_DV),
            pl.BlockSpec((1, GLA_DV_HEAD), lambda i, j: (0, 0)),
            pl.BlockSpec((GLA_DV, d), lambda i, j: (0, 0)),
            tile(d),
            pl.BlockSpec((1, 1, N_MOD, d), lambda i, j: (i, seg(j), 0, 0)),
        ],
        out_specs=tile(d),
        out_shape=jax.ShapeDtypeStruct((b, t, d), F32),
        compiler_params=_cparams("parallel", "parallel"),
        name="gla_out",
    )(o_f, o_b, r, og, wo, xa, mod)


def _pad_run(count):
    return jnp.floor((count + (_RUN - 1)) * (1.0 / _RUN)) * _RUN


def _router_kernel(x_ref, mod_ref, g_ref, wr_ref, br_ref, h_ref, slot_ref, slotc_ref, w_ref,
                   tab_ref, tot_ref, carry_ref):
    i = pl.program_id(0)

    @pl.when(i == 0)
    def _():
        carry_ref[...] = jnp.zeros_like(carry_ref)

    mod = mod_ref[0, 0]
    h = _modulate(x_ref[...], g_ref[...], mod[3:4], mod[4:5])
    h_ref[...] = h.astype(BF16)
    logits = _dot3(h, wr_ref[...]) + br_ref[...]
    lt = logits.T[0:N_EXPERTS]
    eio = lax.broadcasted_iota(jnp.int32, lt.shape, 0)
    vals, hots = [], []
    for kk in range(TOP_K):
        m = jnp.max(lt, axis=0, keepdims=True)
        ik = jnp.min(jnp.where(lt == m, eio, N_EXPERTS), axis=0, keepdims=True)
        hot = eio == ik
        lt = jnp.where(hot, -jnp.inf, lt)
        vals.append(m)
        hots.append(hot)
    es = [jnp.exp(v - vals[0]) for v in vals]
    denom = es[0] + es[1] + es[2] + es[3]
    member = jnp.zeros(lt.shape, F32)
    for hot in hots:
        member = member + hot.astype(F32)
    member_bf = member.astype(BF16)

    r_t = lax.broadcasted_iota(jnp.int32, (_TT, _TT), 0)
    c_t = lax.broadcasted_iota(jnp.int32, (_TT, _TT), 1)
    before = (r_t < c_t).astype(BF16)
    in_run = _dot(member_bf, before)
    run_col = _pad_run(jnp.sum(member, axis=1, keepdims=True))
    r_e = lax.broadcasted_iota(jnp.int32, (N_EXPERTS, N_EXPERTS), 0)
    c_e = lax.broadcasted_iota(jnp.int32, (N_EXPERTS, N_EXPERTS), 1)
    first_col = _dot((c_e < r_e).astype(BF16),
                     jnp.broadcast_to(run_col, (N_EXPERTS, _LANES)).astype(BF16))[:, 0:1]
    slot = first_col + in_run
    rsub = lax.broadcasted_iota(jnp.int32, (_LANES, _TT), 0)
    slot_rows = jnp.zeros((_LANES, _TT), F32)
    w_rows = jnp.zeros((_LANES, _TT), F32)
    for kk in range(TOP_K):
        sk = jnp.sum(jnp.where(hots[kk], slot, 0.0), axis=0, keepdims=True)
        slot_rows = jnp.where(rsub == kk, sk, slot_rows)
        w_rows = jnp.where(rsub == kk, es[kk] / denom, w_rows)
    slot_ref[...] = slot_rows[0:8].astype(jnp.int32)
    slotc_ref[...] = slot_rows.T.astype(jnp.int32)
    w_ref[...] = w_rows.T

    member_pad = jnp.concatenate(
        [member_bf, jnp.zeros((_LANES - N_EXPERTS, _TT), BF16)], axis=0)
    run_row = _pad_run(_dot_nt(jnp.ones((8, _TT), BF16), member_pad))
    r_l = lax.broadcasted_iota(jnp.int32, (_LANES, _LANES), 0)
    c_l = lax.broadcasted_iota(jnp.int32, (_LANES, _LANES), 1)
    first_row = _dot(run_row.astype(BF16), (r_l < c_l).astype(BF16))
    earlier = carry_ref[...]
    tsub = lax.broadcasted_iota(jnp.int32, (8, _LANES), 0)
    tab = jnp.where(tsub == 0, run_row * (1.0 / _RUN),
                    jnp.where(tsub == 1, first_row, jnp.where(tsub == 2, earlier, 0.0)))
    tab_ref[0] = tab.astype(jnp.int32)
    carry_ref[...] = earlier + run_row
    tot_ref[...] = earlier + run_row


def _router(xf, mod, g, wr, br, nt, ntb, xrow, seg):
    d = xf.shape[1]
    n = nt * _TT
    return pl.pallas_call(
        _router_kernel,
        grid=(nt,),
        in_specs=[
            pl.BlockSpec((_TT, d), lambda s: (xrow(s), 0)),
            pl.BlockSpec((1, 1, N_MOD, d), lambda s: (s // ntb, seg(s), 0, 0)),
            pl.BlockSpec((1, d), lambda s: (0, 0)),
            pl.BlockSpec((d, _LANES), lambda s: (0, 0)),
            pl.BlockSpec((1, _LANES), lambda s: (0, 0)),
        ],
        out_specs=[
            pl.BlockSpec((_TT, d), lambda s: (s, 0)),
            pl.BlockSpec((8, _TT), lambda s: (0, s)),
            pl.BlockSpec((_TT, _LANES), lambda s: (s, 0)),
            pl.BlockSpec((_TT, _LANES), lambda s: (s, 0)),
            pl.BlockSpec((1, 8, _LANES), lambda s: (s, 0, 0)),
            pl.BlockSpec((8, _LANES), lambda s: (0, 0)),
        ],
        out_shape=[
            jax.ShapeDtypeStruct((n, d), BF16),
            jax.ShapeDtypeStruct((8, n), jnp.int32),
            jax.ShapeDtypeStruct((n, _LANES), jnp.int32),
            jax.ShapeDtypeStruct((n, _LANES), F32),
            jax.ShapeDtypeStruct((nt, 8, _LANES), jnp.int32),
            jax.ShapeDtypeStruct((8, _LANES), F32),
        ],
        scratch_shapes=[pltpu.VMEM((8, _LANES), F32)],
        compiler_params=_cparams("arbitrary"),
        name="moe_router",
    )(xf, mod, g, wr, br)


def _block_sizes(max_groups):
    return tuple(1 << b for b in range(max_groups.bit_length() - 1, -1, -1))


def _for_each_block(groups, sizes, fn):
    for size in sizes:
        shift = size.bit_length()
        done = ((groups >> shift) << shift) * _RUN

        @pl.when((groups & size) != 0)
        def _():
            fn(done, size * _RUN)


def _for_each_run_block(groups, fn):
    big = groups >> 3

    @pl.when(big > 0)
    def _():
        def body(i, carry):
            fn(i * (8 * _RUN), 8 * _RUN)
            return carry

        lax.fori_loop(0, big, body, 0)

    _for_each_block(groups & 7, (4, 2, 1), lambda off, rows: fn(big * (8 * _RUN) + off, rows))


def _dispatch_kernel(tab_ref, tail_ref, slot_ref, h_ref, xs_ref, xt_ref, used_ref, sem):
    s = pl.program_id(0)
    last = pl.num_programs(0) - 1
    buf = s % 2

    def wait_rows(bi, groups):
        _for_each_block(groups, _block_sizes(_SLOTS // _RUN), lambda off, rows: pltpu.make_async_copy(
            xt_ref.at[bi, pl.ds(0, rows)], xs_ref.at[pl.ds(0, rows)], sem.at[bi]).wait())

    @pl.when(s >= 2)
    def _():
        wait_rows(buf, used_ref[buf])

    slots = slot_ref[...]
    sio = lax.broadcasted_iota(jnp.int32, (_SLOTS, _TT), 0)
    perm = jnp.zeros((_SLOTS, _TT), F32)
    for kk in range(TOP_K):
        perm = perm + (sio == slots[kk:kk + 1]).astype(F32)
    xt_ref[buf] = _dot(perm.astype(BF16), h_ref[...].astype(BF16))

    def issue(e, carry):
        first = tab_ref[0, 1, e]
        base = tab_ref[0, 2, e]

        def block(off, rows):
            pltpu.make_async_copy(
                xt_ref.at[buf, pl.ds(pl.multiple_of(first + off, _RUN), rows)],
                xs_ref.at[pl.ds(pl.multiple_of(base + off, _RUN), rows)], sem.at[buf]).start()

        _for_each_run_block(tab_ref[0, 0, e], block)
        return carry

    lax.fori_loop(0, N_EXPERTS, issue, 0)
    used = tab_ref[0, 1, N_EXPERTS - 1] // _RUN + tab_ref[0, 0, N_EXPERTS - 1]
    used_ref[buf] = used

    @pl.when(s == last)
    def _():
        @pl.when(s >= 1)
        def _():
            wait_rows(1 - buf, used_ref[1 - buf])

        wait_rows(buf, used)
        xt_ref[0, 0:_TM, :] = jnp.zeros((_TM, xt_ref.shape[2]), F32)

        def tail_blocks(e, act):
            base = tail_ref[1, e]

            def block(off, rows):
                act(pltpu.make_async_copy(
                    xt_ref.at[0, pl.ds(0, rows)],
                    xs_ref.at[pl.ds(pl.multiple_of(base + off, _RUN), rows)], sem.at[0]))

            _for_each_block(tail_ref[0, e], _block_sizes(_TM // _RUN - 1), block)

        def issue_tail(e, carry):
            tail_blocks(e, lambda cp: cp.start())
            return carry

        def drain_tail(e, carry):
            tail_blocks(e, lambda cp: cp.wait())
            return carry

        lax.fori_loop(0, N_EXPERTS, issue_tail, 0)
        lax.fori_loop(0, N_EXPERTS, drain_tail, 0)

        def free_tile(i):
            row0 = pl.multiple_of(tail_ref[2, 1] + i * _TM, _TM)
            return pltpu.make_async_copy(xt_ref.at[0, pl.ds(0, _TM)], xs_ref.at[pl.ds(row0, _TM)],
                                         sem.at[0])

        def issue_free(i, carry):
            free_tile(i).start()
            return carry

        def drain_free(i, carry):
            free_tile(i).wait()
            return carry

        lax.fori_loop(0, tail_ref[2, 0], issue_free, 0)
        lax.fori_loop(0, tail_ref[2, 0], drain_free, 0)


def _dispatch(tab, tail, slot, h, rows):
    n, d = h.shape
    nt = n // _TT
    return pl.pallas_call(
        _dispatch_kernel,
        grid=(nt,),
        in_specs=[
            pl.BlockSpec((1, 8, _LANES), lambda s: (s, 0, 0), memory_space=pltpu.SMEM),
            pl.BlockSpec((8, _LANES), lambda s: (0, 0), memory_space=pltpu.SMEM),
            pl.BlockSpec((8, _TT), lambda s: (0, s)),
            pl.BlockSpec((_TT, d), lambda s: (s, 0)),
        ],
        out_specs=pl.BlockSpec(memory_space=pl.ANY),
        out_shape=jax.ShapeDtypeStruct((rows, d), F32),
        scratch_shapes=[pltpu.VMEM((2, _SLOTS, d), F32), pltpu.SMEM((2,), jnp.int32),
                        pltpu.SemaphoreType.DMA((2,))],
        compiler_params=_cparams("arbitrary"),
        name="moe_dispatch",
    )(tab, tail, slot, h)


def _expert_kernel(te_ref, nu_ref, xs_ref, wgu_ref, bgu_ref, wd_ref, bd_ref, ys_ref,
                   wgu_bf_ref, wd_bf_ref):
    t = pl.program_id(0)
    live = t < nu_ref[0]
    new_expert = jnp.logical_or(t == 0, te_ref[t] != te_ref[jnp.maximum(t - 1, 0)])

    @pl.when(jnp.logical_and(live, new_expert))
    def _():
        wgu_bf_ref[...] = wgu_ref[0, 0].astype(BF16)
        wd_bf_ref[...] = wd_ref[0, 0].astype(BF16)

    @pl.when(live)
    def _():
        x = xs_ref[...].astype(BF16)
        gu = _dot(x, wgu_bf_ref[...]) + bgu_ref[0, 0]
        gate = jnp.minimum(gu[:, 0:D_EXPERT], SWIGLU_LIMIT)
        up = jnp.clip(gu[:, D_EXPERT:], -SWIGLU_LIMIT, SWIGLU_LIMIT)
        act = (up + 1.0) * (gate * _sigmoid(SWIGLU_ALPHA * gate))
        ys_ref[...] = _dot(act.astype(BF16), wd_bf_ref[...]) + bd_ref[0, 0]

    @pl.when(jnp.logical_not(live))
    def _():
        ys_ref[...] = jnp.zeros_like(ys_ref)


def _experts(tile_expert, n_used, xs, layer, wgu, bgu, wd, bd):
    p, d = xs.shape
    nt = p // _TM
    depth, e, _, n2 = wgu.shape
    grid_spec = pltpu.PrefetchScalarGridSpec(
        num_scalar_prefetch=2,
        grid=(nt,),
        in_specs=[
            pl.BlockSpec((_TM, d), lambda t, te, nu: (jnp.minimum(t, nu[0] - 1), 0)),
            pl.BlockSpec((1, 1, d, n2), lambda t, te, nu: (layer, te[t], 0, 0)),
            pl.BlockSpec((1, 1, 1, n2), lambda t, te, nu: (layer, te[t], 0, 0)),
            pl.BlockSpec((1, 1, D_EXPERT, d), lambda t, te, nu: (layer, te[t], 0, 0)),
            pl.BlockSpec((1, 1, 1, d), lambda t, te, nu: (layer, te[t], 0, 0)),
        ],
        out_specs=pl.BlockSpec((_TM, d), lambda t, te, nu: (t, 0)),
        scratch_shapes=[pltpu.VMEM((d, n2), BF16), pltpu.VMEM((D_EXPERT, d), BF16)],
    )
    return pl.pallas_call(
        _expert_kernel,
        grid_spec=grid_spec,
        out_shape=jax.ShapeDtypeStruct((p, d), F32),
        compiler_params=_cparams("arbitrary"),
        name="moe_experts",
    )(tile_expert, n_used, xs, wgu, bgu.reshape(depth, e, 1, n2), wd, bd.reshape(depth, e, 1, d))


def _combine_kernel(tab_ref, tabn_ref, slotc_ref, w_ref, ys_ref, x_ref, mod_ref, o_ref, yb_ref, sem):
    s = pl.program_id(0)
    buf = s % 2

    def issue_runs(t_ref, bi):
        def issue(e, carry):
            first = t_ref[0, 1, e]
            base = t_ref[0, 2, e]

            def block(off, rows):
                pltpu.make_async_copy(
                    ys_ref.at[pl.ds(pl.multiple_of(base + off, _RUN), rows)],
                    yb_ref.at[bi, pl.ds(pl.multiple_of(first + off, _RUN), rows)], sem.at[bi]).start()

            _for_each_run_block(t_ref[0, 0, e], block)
            return carry

        lax.fori_loop(0, N_EXPERTS, issue, 0)

    @pl.when(s == 0)
    def _():
        yb_ref[...] = jnp.zeros_like(yb_ref)
        issue_runs(tab_ref, 0)

    @pl.when(s + 1 < pl.num_programs(0))
    def _():
        issue_runs(tabn_ref, 1 - buf)

    used = tab_ref[0, 1, N_EXPERTS - 1] + tab_ref[0, 0, N_EXPERTS - 1] * _RUN
    _for_each_block(used // _RUN, _block_sizes(_SLOTS // _RUN), lambda off, rows: pltpu.make_async_copy(
        ys_ref.at[pl.ds(0, rows)], yb_ref.at[buf, pl.ds(0, rows)], sem.at[buf]).wait())
    y = yb_ref[buf].astype(BF16)
    lio = lax.broadcasted_iota(jnp.int32, (_TT, _SLOTS), 1)
    slotc = slotc_ref[...]
    w = w_ref[...]
    gates = jnp.zeros((_TT, _SLOTS), F32)
    for kk in range(TOP_K):
        gates = gates + jnp.where(lio == slotc[:, kk:kk + 1], w[:, kk:kk + 1], 0.0)
    f = _dot(gates.astype(BF16), y)
    o_ref[...] = x_ref[...] + mod_ref[0, 0][5:6] * f


def _combine(tab, slotc, w, ys, xf, mod, nt, ntb, xrow, seg):
    d = xf.shape[1]
    n = nt * _TT
    return pl.pallas_call(
        _combine_kernel,
        grid=(nt,),
        in_specs=[
            pl.BlockSpec((1, 8, _LANES), lambda s: (s, 0, 0), memory_space=pltpu.SMEM),
            pl.BlockSpec((1, 8, _LANES), lambda s: (jnp.minimum(s + 1, nt - 1), 0, 0),
                         memory_space=pltpu.SMEM),
            pl.BlockSpec((_TT, _LANES), lambda s: (s, 0)),
            pl.BlockSpec((_TT, _LANES), lambda s: (s, 0)),
            pl.BlockSpec(memory_space=pl.ANY),
            pl.BlockSpec((_TT, d), lambda s: (xrow(s), 0)),
            pl.BlockSpec((1, 1, N_MOD, d), lambda s: (s // ntb, seg(s), 0, 0)),
        ],
        out_specs=pl.BlockSpec((_TT, d), lambda s: (s, 0)),
        out_shape=jax.ShapeDtypeStruct((n, d), F32),
        scratch_shapes=[pltpu.VMEM((2, _SLOTS, d), F32), pltpu.SemaphoreType.DMA((2,))],
        compiler_params=_cparams("arbitrary"),
        name="moe_combine",
    )(tab, tab, slotc, w, ys, xf, mod)


def _moe(xa, mod, g2, wr, br, layer, wgu, bgu, wd, bd, nct, skip=0):
    b, t, d = xa.shape
    ntb_all = t // _TT
    ntb = ntb_all - skip
    nt = b * ntb
    n = nt * _TT
    xf = xa.reshape(b * t, d)
    xrow = lambda s: (s // ntb) * ntb_all + skip + s % ntb
    seg = lambda s: jnp.where(skip + s % ntb >= nct, 1, 0)
    h, slot, slotc, w, tab, tot = _router(xf, mod, g2, wr, br, nt, ntb, xrow, seg)
    run_rows = tot[0].astype(jnp.int32)
    padded = (run_rows + _TM - 1) // _TM * _TM
    ends = jnp.cumsum(padded)
    starts = ends - padded
    row = lax.broadcasted_iota(jnp.int32, (8, _LANES), 0)
    tab = tab + jnp.where(row == 2, starts[None, :], 0)[None]
    max_rows = n * TOP_K + nt * N_EXPERTS * (_RUN - 1) + N_EXPERTS * (_TM - 1)
    n_tiles = max_rows // _TM
    lane = lax.broadcasted_iota(jnp.int32, (8, _LANES), 1)
    used_rows = ends[N_EXPERTS - 1]
    free = jnp.where(lane == 0, n_tiles - used_rows // _TM, jnp.where(lane == 1, used_rows, 0))
    tail = jnp.where(row == 0, ((padded - run_rows) // _RUN)[None, :],
                     jnp.where(row == 1, (starts + run_rows)[None, :],
                               jnp.where(row == 2, free, 0)))
    tile_row = jnp.arange(n_tiles, dtype=jnp.int32) * _TM
    tile_expert = jnp.minimum(jnp.sum(tile_row[:, None] >= ends[None, :N_EXPERTS], axis=1),
                              N_EXPERTS - 1).astype(jnp.int32)
    n_used = (ends[N_EXPERTS - 1:N_EXPERTS] // _TM).astype(jnp.int32)
    xs = _dispatch(tab, tail, slot, h, n_tiles * _TM)
    ys = _experts(tile_expert, n_used, xs, layer, wgu, bgu, wd, bd)
    return _combine(tab, slotc, w, ys, xf, mod, nt, ntb, xrow, seg).reshape(b, ntb * _TT, d)


def _rope_tables(n_ctx, n_lat):
    rows = n_lat // GRID_W
    row_ids = jnp.repeat(jnp.arange(rows), GRID_W).astype(F32)
    col_ids = jnp.tile(jnp.arange(GRID_W), rows).astype(F32)
    axis_dim = HEAD_DIM // 2
    inv_freq = 1.0 / (ROPE_THETA ** (jnp.arange(0, axis_dim, 2, dtype=F32) / axis_dim))
    ang = jnp.concatenate([row_ids[:, None] * inv_freq, col_ids[:, None] * inv_freq], axis=-1)
    cos = jnp.concatenate([jnp.ones((n_ctx, axis_dim), F32), jnp.cos(ang)], axis=0)
    sin = jnp.concatenate([jnp.zeros((n_ctx, axis_dim), F32), jnp.sin(ang)], axis=0)
    return cos.T, sin.T


def kernel(x, c, ctx, c_ctx, ada_w, ada_b, norm1_g, norm2_g, attn_w_qkv, attn_q_gain, attn_k_gain, attn_w_o, gla_w_in, gla_w_gk2_f, gla_b_gk_f, gla_w_gk2_b, gla_b_gk_b, gla_o_gain, gla_w_o, moe_w_router, moe_b_router, moe_w_gu, moe_b_gu, moe_w_down, moe_b_down):
    b, n_lat, d = x.shape
    n_ctx = ctx.shape[1]
    assert d == D_MODEL and n_ctx % _TT == 0 and n_lat % _TT == 0 and n_lat % GRID_W == 0
    nct = n_ctx // _TT
    depth = ada_w.shape[0]

    r = -(-(b + 1) // 8) * 8
    cond = jnp.zeros((r, d), F32).at[:b].set(c).at[b].set(c_ctx)
    table = _ada_table(cond, ada_w, ada_b).reshape(depth, r, N_MOD, d)
    mods = jnp.stack([jnp.broadcast_to(table[:, b:b + 1], (depth, b, N_MOD, d)), table[:, :b]],
                     axis=2)

    cos_t, sin_t = _rope_tables(n_ctx, n_lat)
    perm = jnp.concatenate([jnp.arange(0, HEAD_DIM, 2), jnp.arange(1, HEAD_DIM, 2)])
    qk_cols = (jnp.arange(N_Q_HEADS + N_KV_HEADS)[:, None] * HEAD_DIM + perm[None, :]).reshape(-1)
    cols = jnp.concatenate([qk_cols, jnp.arange(Q_DIM + KV_DIM, QKV_DIM)])

    xa = jnp.concatenate([ctx, x], axis=1)
    for i in range(depth):
        mod = mods[i]
        j = i // N_MIXERS
        g1 = norm1_g[i].reshape(1, d)
        if i % N_MIXERS == 0:
            wt = attn_w_qkv[j][:, cols].T.astype(BF16)
            gq = attn_q_gain[j][perm].reshape(HEAD_DIM, 1)
            gk = attn_k_gain[j][perm].reshape(HEAD_DIM, 1)
            qt, k, vt = _qkv_project(xa, mod, g1, wt, gq, gk, cos_t, sin_t, nct)
            xa = _attention(qt, k, vt, attn_w_o[j].astype(BF16), xa, mod, nct)
        else:
            n_main = 2 * GLA_DK + 2 * GLA_DV
            wm = gla_w_in[j][:, :n_main].astype(BF16)
            wl = jnp.zeros((d, _LANES), F32).at[:, :2 * GLA_GATE_RANK].set(
                gla_w_in[j][:, n_main:]).astype(BF16)
            w2 = jnp.zeros((_LANES, 2 * GLA_DK), F32)
            w2 = w2.at[:GLA_GATE_RANK, :GLA_DK].set(gla_w_gk2_f[j])
            w2 = w2.at[GLA_GATE_RANK:2 * GLA_GATE_RANK, GLA_DK:].set(gla_w_gk2_b[j])
            b2 = jnp.concatenate([gla_b_gk_f[j], gla_b_gk_b[j]]).reshape(1, 2 * GLA_DK)
            q, k, v, rr, gf, gb = _gla_project(xa, mod, g1, wm, wl, w2, b2, nct)
            o_f, o_b = _gla_scan(q, k, v, gf, gb, nct)
            xa = _gla_readout(o_f, o_b, rr, gla_o_gain[j].reshape(1, GLA_DV_HEAD),
                              gla_w_o[j].astype(BF16), xa, mod, nct)
        wr = jnp.zeros((d, _LANES), F32).at[:, :N_EXPERTS].set(moe_w_router[i])
        br = jnp.zeros((1, _LANES), F32).at[0, :N_EXPERTS].set(moe_b_router[i])
        if i == depth - 1:
            return _moe(xa, mod, norm2_g[i].reshape(1, d), wr, br,
                        i, moe_w_gu, moe_b_gu, moe_w_down, moe_b_down, nct, skip=nct)
        xa = _moe(xa, mod, norm2_g[i].reshape(1, d), wr, br,
                  i, moe_w_gu, moe_b_gu, moe_w_down, moe_b_down, nct)
    return xa[:, n_ctx:]
```

```python
import functools

import jax
import jax.numpy as jnp
from jax import lax
from jax.experimental import pallas as pl
from jax.experimental.pallas import tpu as pltpu

F32 = jnp.float32
BF16 = jnp.bfloat16

D_MODEL = 1024
DEPTH = 4
GRID_W = 64
N_MIXERS = 2
N_MOD = 6
RMS_EPS = 1e-6
HEAD_DIM = 64
N_Q_HEADS = D_MODEL // HEAD_DIM
N_KV_HEADS = N_Q_HEADS // 4
GQA_GROUP = N_Q_HEADS // N_KV_HEADS
Q_DIM = N_Q_HEADS * HEAD_DIM
KV_DIM = N_KV_HEADS * HEAD_DIM
QKV_DIM = Q_DIM + 2 * KV_DIM
V_ROWS = HEAD_DIM + 16
LOG2E = 1.4426950408889634
ROPE_THETA = 10000.0
GLA_HEADS = 4
GLA_DK = D_MODEL // 2
GLA_DV = D_MODEL
GLA_DK_HEAD = GLA_DK // GLA_HEADS
GLA_DV_HEAD = GLA_DV // GLA_HEADS
GLA_GATE_RANK = 16
GLA_GATE_NORM = 16.0
N_EXPERTS = 32
TOP_K = 4
D_EXPERT = D_MODEL
SWIGLU_LIMIT = 7.0
SWIGLU_ALPHA = 1.702

_LANES = 128
_TT = 256
_KC = 256
_HPI = 2
_GLA_SAFE_LOG_DECAY = 60.0
_TM = 512
_RUN = 8
_SLOTS = _TT * TOP_K + N_EXPERTS * _RUN
_ADA_TN = 1536
_VMEM_LIMIT = 56 * 1024 * 1024


def _cparams(*sem):
    return pltpu.CompilerParams(dimension_semantics=sem, vmem_limit_bytes=_VMEM_LIMIT)


def _dot(a, b):
    return jnp.dot(a, b, preferred_element_type=F32)


def _dot_nt(a, b):
    return lax.dot_general(a, b, (((1,), (1,)), ((), ())), preferred_element_type=F32)


def _split_bf16(a):
    hi = a.astype(BF16)
    lo = (a - hi.astype(F32)).astype(BF16)
    return hi, lo


def _dot3(a, w):
    ah, al = _split_bf16(a)
    wh, wl = _split_bf16(w)
    return _dot(ah, wh) + _dot(ah, wl) + _dot(al, wh)


def _modulate(x, g, shift, scale):
    ms = jnp.mean(x * x, axis=-1, keepdims=True)
    y = x * lax.rsqrt(ms + RMS_EPS) * g
    return y * (1.0 + scale) + shift


def _sigmoid(x):
    return 1.0 / (1.0 + jnp.exp(-x))


def _ada_kernel(cond_ref, w_ref, b_ref, o_ref):
    c = cond_ref[...]
    o_ref[0] = _dot3(c * _sigmoid(c), w_ref[0]) + b_ref[0]


def _ada_table(cond, ada_w, ada_b):
    depth, d, n = ada_w.shape
    r = cond.shape[0]
    return pl.pallas_call(
        _ada_kernel,
        grid=(depth, n // _ADA_TN),
        in_specs=[
            pl.BlockSpec((r, d), lambda i, j: (0, 0)),
            pl.BlockSpec((1, d, _ADA_TN), lambda i, j: (i, 0, j)),
            pl.BlockSpec((1, 1, _ADA_TN), lambda i, j: (i, 0, j)),
        ],
        out_specs=pl.BlockSpec((1, r, _ADA_TN), lambda i, j: (i, 0, j)),
        out_shape=jax.ShapeDtypeStruct((depth, r, n), F32),
        compiler_params=_cparams("parallel", "parallel"),
        name="ada_table",
    )(cond, ada_w, ada_b.reshape(depth, 1, n))


def _qkv_kernel(x_ref, mod_ref, g_ref, wt_ref, gq_ref, gk_ref, cos_ref, sin_ref,
                qt_ref, k_ref, vt_ref):
    mod = mod_ref[0, 0]
    h = _modulate(x_ref[0], g_ref[...], mod[0:1], mod[1:2])
    ht = h.T.astype(BF16)
    qkvt = _dot(wt_ref[...], ht)
    c = cos_ref[...]
    s = sin_ref[...]
    half = HEAD_DIM // 2

    def norm_rope(blk, gain, scale):
        ms = jnp.mean(blk * blk, axis=0, keepdims=True)
        n = blk * lax.rsqrt(ms + RMS_EPS) * gain
        x1 = n[0:half]
        x2 = n[half:HEAD_DIM]
        return jnp.concatenate([x1 * c - x2 * s, x1 * s + x2 * c], axis=0) * scale

    gq = gq_ref[...]
    gk = gk_ref[...]
    for hh in range(N_Q_HEADS):
        r0 = hh * HEAD_DIM
        qt_ref[0, r0:r0 + HEAD_DIM, :] = norm_rope(
            qkvt[r0:r0 + HEAD_DIM], gq, LOG2E * HEAD_DIM ** -0.5).astype(BF16)
    kt = jnp.concatenate(
        [norm_rope(qkvt[Q_DIM + j * HEAD_DIM:Q_DIM + (j + 1) * HEAD_DIM], gk, 1.0)
         for j in range(N_KV_HEADS)], axis=0)
    k_ref[0] = kt.T.astype(BF16)
    ones = jnp.ones((V_ROWS - HEAD_DIM, vt_ref.shape[2]), BF16)
    for j in range(N_KV_HEADS):
        v0 = Q_DIM + KV_DIM + j * HEAD_DIM
        vt_ref[0, j * V_ROWS:j * V_ROWS + HEAD_DIM, :] = qkvt[v0:v0 + HEAD_DIM].astype(BF16)
        vt_ref[0, j * V_ROWS + HEAD_DIM:(j + 1) * V_ROWS, :] = ones


def _qkv_project(xa, mod, g, wt, gq, gk, cos_t, sin_t, nct):
    b, t, d = xa.shape
    nt = t // _TT
    seg = lambda j: jnp.where(j >= nct, 1, 0)
    return pl.pallas_call(
        _qkv_kernel,
        grid=(b, nt),
        in_specs=[
            pl.BlockSpec((1, _TT, d), lambda i, j: (i, j, 0)),
            pl.BlockSpec((1, 1, N_MOD, d), lambda i, j: (i, seg(j), 0, 0)),
            pl.BlockSpec((1, d), lambda i, j: (0, 0)),
            pl.BlockSpec((QKV_DIM, d), lambda i, j: (0, 0)),
            pl.BlockSpec((HEAD_DIM, 1), lambda i, j: (0, 0)),
            pl.BlockSpec((HEAD_DIM, 1), lambda i, j: (0, 0)),
            pl.BlockSpec((HEAD_DIM // 2, _TT), lambda i, j: (0, j)),
            pl.BlockSpec((HEAD_DIM // 2, _TT), lambda i, j: (0, j)),
        ],
        out_specs=[
            pl.BlockSpec((1, Q_DIM, _TT), lambda i, j: (i, 0, j)),
            pl.BlockSpec((1, _TT, KV_DIM), lambda i, j: (i, j, 0)),
            pl.BlockSpec((1, N_KV_HEADS * V_ROWS, _TT), lambda i, j: (i, 0, j)),
        ],
        out_shape=[
            jax.ShapeDtypeStruct((b, Q_DIM, t), BF16),
            jax.ShapeDtypeStruct((b, t, KV_DIM), BF16),
            jax.ShapeDtypeStruct((b, N_KV_HEADS * V_ROWS, t), BF16),
        ],
        compiler_params=_cparams("parallel", "parallel"),
        name="attn_qkv",
    )(xa, mod, g, wt, gq, gk, cos_t, sin_t)


def _attn_kernel(qt_ref, k_ref, vt_ref, wo_ref, x_ref, mod_ref, o_ref, qe_ref, ot_ref,
                 *, nct, n_ctx):
    j = pl.program_id(1)
    t_all = k_ref.shape[1]

    def heads(nk):
        def head_pair(hp, carry):
            kvh = hp // (GQA_GROUP // _HPI)
            k0 = pl.multiple_of(kvh * HEAD_DIM, HEAD_DIM)
            v0 = pl.multiple_of(kvh * V_ROWS, 16)
            for u in range(_HPI):
                q0 = pl.multiple_of((_HPI * hp + u) * HEAD_DIM, HEAD_DIM)
                qe_ref[u] = jnp.zeros(qe_ref.shape[1:], BF16)
                qe_ref[u, pl.ds(k0, HEAD_DIM), :] = qt_ref[0, pl.ds(q0, HEAD_DIM), :]
            bounds = list(range(0, nk, _KC)) + [nk]
            chunks = list(zip(bounds[:-1], bounds[1:]))

            st = [_dot(k_ref[0, 0:nk, :], qe_ref[u]) for u in range(_HPI)]
            acc = [jnp.zeros((V_ROWS, _TT), F32) for _ in range(_HPI)]
            m_run = [jnp.full((1, _TT), -jnp.inf, F32) for _ in range(_HPI)]
            for lo, hi in chunks:
                for u in range(_HPI):
                    s_cur = st[u][lo:hi]
                    m_new = jnp.maximum(m_run[u], jnp.max(s_cur, axis=0, keepdims=True))
                    p = jnp.exp2(s_cur - m_new).astype(BF16)
                    pv = _dot(vt_ref[0, pl.ds(v0, V_ROWS), lo:hi], p)
                    acc[u] = acc[u] * jnp.exp2(m_run[u] - m_new) + pv
                    m_run[u] = m_new
            for u in range(_HPI):
                q0 = pl.multiple_of((_HPI * hp + u) * HEAD_DIM, HEAD_DIM)
                ot_ref[pl.ds(q0, HEAD_DIM), :] = acc[u][0:HEAD_DIM] / acc[u][HEAD_DIM:HEAD_DIM + 1]
            return carry

        lax.fori_loop(0, N_Q_HEADS // _HPI, head_pair, 0)

    @pl.when(j < nct)
    def _():
        heads(n_ctx)

    @pl.when(j >= nct)
    def _():
        heads(t_all)

    att = ot_ref[...].T.astype(BF16)
    out = _dot(att, wo_ref[...])
    o_ref[0] = x_ref[0] + mod_ref[0, 0][2:3] * out


def _attention(qt, k, vt, wo, xa, mod, nct):
    b, t, d = xa.shape
    nt = t // _TT
    seg = lambda j: jnp.where(j >= nct, 1, 0)
    kern = functools.partial(_attn_kernel, nct=nct, n_ctx=nct * _TT)
    return pl.pallas_call(
        kern,
        grid=(b, nt),
        in_specs=[
            pl.BlockSpec((1, Q_DIM, _TT), lambda i, j: (i, 0, j)),
            pl.BlockSpec((1, t, KV_DIM), lambda i, j: (i, 0, 0)),
            pl.BlockSpec((1, N_KV_HEADS * V_ROWS, t), lambda i, j: (i, 0, 0)),
            pl.BlockSpec((Q_DIM, d), lambda i, j: (0, 0)),
            pl.BlockSpec((1, _TT, d), lambda i, j: (i, j, 0)),
            pl.BlockSpec((1, 1, N_MOD, d), lambda i, j: (i, seg(j), 0, 0)),
        ],
        out_specs=pl.BlockSpec((1, _TT, d), lambda i, j: (i, j, 0)),
        out_shape=jax.ShapeDtypeStruct((b, t, d), F32),
        scratch_shapes=[pltpu.VMEM((_HPI, KV_DIM, _TT), BF16), pltpu.VMEM((Q_DIM, _TT), F32)],
        compiler_params=_cparams("parallel", "arbitrary"),
        name="attn_core",
    )(qt, k, vt, wo, xa, mod)


def _gla_in_kernel(x_ref, mod_ref, g_ref, wm_ref, wl_ref, w2_ref, b2_ref,
                   q_ref, k_ref, v_ref, r_ref, gf_ref, gb_ref):
    mod = mod_ref[0, 0]
    h = _modulate(x_ref[0], g_ref[...], mod[0:1], mod[1:2]).astype(BF16)
    main = _dot(h, wm_ref[...])
    q_ref[0] = (main[:, 0:GLA_DK] * GLA_DK_HEAD ** -0.5).astype(BF16)
    k_ref[0] = main[:, GLA_DK:2 * GLA_DK].astype(BF16)
    v_ref[0] = main[:, 2 * GLA_DK:2 * GLA_DK + GLA_DV].astype(BF16)
    r_ref[0] = main[:, 2 * GLA_DK + GLA_DV:].astype(BF16)
    low = _dot(h, wl_ref[...])
    pre = _dot3(low, w2_ref[...]) + b2_ref[...]
    logsig = jnp.minimum(pre, 0.0) - jnp.log(1.0 + jnp.exp(-jnp.abs(pre)))
    gate = logsig / GLA_GATE_NORM
    gf_ref[0] = gate[:, 0:GLA_DK]
    gb_ref[0] = gate[:, GLA_DK:]


def _gla_project(xa, mod, g, wm, wl, w2, b2, nct):
    b, t, d = xa.shape
    nt = t // _TT
    seg = lambda j: jnp.where(j >= nct, 1, 0)
    tile = lambda n: pl.BlockSpec((1, _TT, n), lambda i, j: (i, j, 0))
    full = lambda a: pl.BlockSpec(a.shape, lambda i, j: (0,) * a.ndim)
    return pl.pallas_call(
        _gla_in_kernel,
        grid=(b, nt),
        in_specs=[
            tile(d),
            pl.BlockSpec((1, 1, N_MOD, d), lambda i, j: (i, seg(j), 0, 0)),
            full(g), full(wm), full(wl), full(w2), full(b2),
        ],
        out_specs=[tile(GLA_DK), tile(GLA_DK), tile(GLA_DV), tile(GLA_DV),
                   tile(GLA_DK), tile(GLA_DK)],
        out_shape=[
            jax.ShapeDtypeStruct((b, t, GLA_DK), BF16),
            jax.ShapeDtypeStruct((b, t, GLA_DK), BF16),
            jax.ShapeDtypeStruct((b, t, GLA_DV), BF16),
            jax.ShapeDtypeStruct((b, t, GLA_DV), BF16),
            jax.ShapeDtypeStruct((b, t, GLA_DK), F32),
            jax.ShapeDtypeStruct((b, t, GLA_DK), F32),
        ],
        compiler_params=_cparams("parallel", "parallel"),
        name="gla_in",
    )(xa, mod, g, wm, wl, w2, b2)


def _gla_scan_kernel(qf_ref, kf_ref, vf_ref, gf_ref, qb_ref, kb_ref, vb_ref, gb_ref,
                     tril_ref, triu_ref, of_ref, ob_ref, st_ref, a_ref, kx_ref, bx_ref):
    @pl.when(pl.program_id(1) == 0)
    def _():
        st_ref[...] = jnp.zeros_like(st_ref)

    chains = []
    for h in range(GLA_HEADS):
        kcols = slice(h * GLA_DK_HEAD, (h + 1) * GLA_DK_HEAD)
        vcols = slice(h * GLA_DV_HEAD, (h + 1) * GLA_DV_HEAD)
        chains.append((qf_ref, kf_ref, vf_ref, gf_ref, tril_ref, of_ref, kcols, vcols, False))
        chains.append((qb_ref, kb_ref, vb_ref, gb_ref, triu_ref, ob_ref, kcols, vcols, True))

    def cumulative(g_ref, tri_ref):
        gh, gl = _split_bf16(g_ref[0])
        return _dot(tri_ref[...], gh) + _dot(tri_ref[...], gl)

    bcum_all = {False: cumulative(gf_ref, tril_ref), True: cumulative(gb_ref, triu_ref)}
    work = []
    for ci, (q_ref, k_ref, v_ref, g_ref, tri_ref, o_ref, kcols, vcols, reverse) in enumerate(chains):
        bcum = bcum_all[reverse][:, kcols]
        b_end = bcum[0:1] if reverse else bcum[_TT - 1:_TT]
        q = q_ref[0, :, kcols].astype(F32)
        k = k_ref[0, :, kcols].astype(F32)
        qd = (q * jnp.exp(bcum)).astype(BF16)
        kd = (k * jnp.exp(b_end - bcum)).astype(BF16)
        a_ref[ci] = _dot_nt(qd, (k * jnp.exp(-bcum)).astype(BF16))
        work.append((q, k, bcum, b_end, qd, kd))

    for ci, (q, k, bcum, b_end, qd, kd) in enumerate(work):
        @pl.when(jnp.max(-b_end) > _GLA_SAFE_LOG_DECAY)
        def _():
            kx_ref[...] = k
            bx_ref[...] = bcum
            lane = lax.broadcasted_iota(jnp.int32, (_TT, _TT), 1)

            def column(j, carry):
                decay = jnp.exp(jnp.minimum(bcum - bx_ref[pl.ds(j, 1), :], 0.0))
                col = jnp.sum(q * decay * kx_ref[pl.ds(j, 1), :], axis=1, keepdims=True)
                a_ref[ci] = jnp.where(lane == j, col, a_ref[ci])
                return carry

            lax.fori_loop(0, _TT, column, 0)

    for ci, (q_ref, k_ref, v_ref, g_ref, tri_ref, o_ref, kcols, vcols, reverse) in enumerate(chains):
        q, k, bcum, b_end, qd, kd = work[ci]
        v = v_ref[0, :, vcols]
        a = jnp.where(tri_ref[...] > 0, a_ref[ci], 0.0).astype(BF16)
        st = st_ref[ci]
        o_ref[0, :, vcols] = (_dot_nt(qd, st.astype(BF16)) + _dot(a, v)).astype(BF16)
        vt = v.astype(F32).T.astype(BF16)
        st_ref[ci] = st * jnp.exp(b_end) + _dot(vt, kd)


def _gla_scan(q, k, v, gf, gb, nct):
    b, t, _ = q.shape
    nb = t // _TT

    def rev(s):
        return jnp.where(s < nct, nct - 1 - s, nb - 1 - (s - nct))

    kspec = pl.BlockSpec((1, _TT, GLA_DK), lambda i, s: (i, s, 0))
    vspec = pl.BlockSpec((1, _TT, GLA_DV), lambda i, s: (i, s, 0))
    kspec_r = pl.BlockSpec((1, _TT, GLA_DK), lambda i, s: (i, rev(s), 0))
    vspec_r = pl.BlockSpec((1, _TT, GLA_DV), lambda i, s: (i, rev(s), 0))
    tspec = pl.BlockSpec((_TT, _TT), lambda i, s: (0, 0))
    row = lax.broadcasted_iota(jnp.int32, (_TT, _TT), 0)
    col = lax.broadcasted_iota(jnp.int32, (_TT, _TT), 1)
    n_chains = 2 * GLA_HEADS
    return pl.pallas_call(
        _gla_scan_kernel,
        grid=(b, nb),
        in_specs=[kspec, kspec, vspec, kspec, kspec_r, kspec_r, vspec_r, kspec_r, tspec, tspec],
        out_specs=[vspec, vspec_r],
        out_shape=[jax.ShapeDtypeStruct((b, t, GLA_DV), BF16)] * 2,
        scratch_shapes=[pltpu.VMEM((n_chains, GLA_DV_HEAD, GLA_DK_HEAD), F32),
                        pltpu.VMEM((n_chains, _TT, _TT), F32),
                        pltpu.VMEM((_TT, GLA_DK_HEAD), F32),
                        pltpu.VMEM((_TT, GLA_DK_HEAD), F32)],
        compiler_params=_cparams("parallel", "arbitrary"),
        name="gla_scan",
    )(q, k, v, gf, q, k, v, gb, (col <= row).astype(BF16), (col >= row).astype(BF16))


def _gla_out_kernel(of_ref, ob_ref, r_ref, og_ref, wo_ref, x_ref, mod_ref, o_ref):
    o = of_ref[0].astype(F32) + ob_ref[0].astype(F32)
    og = og_ref[...]
    parts = []
    for h in range(GLA_HEADS):
        oh = o[:, h * GLA_DV_HEAD:(h + 1) * GLA_DV_HEAD]
        ms = jnp.mean(oh * oh, axis=-1, keepdims=True)
        parts.append(oh * lax.rsqrt(ms + RMS_EPS) * og)
    r = r_ref[0].astype(F32)
    y = jnp.concatenate(parts, axis=1) * (r * _sigmoid(r))
    out = _dot(y.astype(BF16), wo_ref[...])
    o_ref[0] = x_ref[0] + mod_ref[0, 0][2:3] * out


def _gla_readout(o_f, o_b, r, og, wo, xa, mod, nct):
    b, t, d = xa.shape
    nt = t // _TT
    seg = lambda j: jnp.where(j >= nct, 1, 0)
    tile = lambda n: pl.BlockSpec((1, _TT, n), lambda i, j: (i, j, 0))
    return pl.pallas_call(
        _gla_out_kernel,
        grid=(b, nt),
        in_specs=[
            tile(GLA_DV), tile(GLA_DV), tile(GLA_DV),
            pl.BlockSpec((1, GLA_DV_HEAD), lambda i, j: (0, 0)),
            pl.BlockSpec((GLA_DV, d), lambda i, j: (0, 0)),
            tile(d),
            pl.BlockSpec((1, 1, N_MOD, d), lambda i, j: (i, seg(j), 0, 0)),
        ],
        out_specs=tile(d),
        out_shape=jax.ShapeDtypeStruct((b, t, d), F32),
        compiler_params=_cparams("parallel", "parallel"),
        name="gla_out",
    )(o_f, o_b, r, og, wo, xa, mod)


def _pad_run(count):
    return jnp.floor((count + (_RUN - 1)) * (1.0 / _RUN)) * _RUN


def _router_kernel(x_ref, mod_ref, g_ref, wr_ref, br_ref, h_ref, slot_ref, slotc_ref, w_ref,
                   tab_ref, tot_ref, carry_ref):
    i = pl.program_id(0)

    @pl.when(i == 0)
    def _():
        carry_ref[...] = jnp.zeros_like(carry_ref)

    mod = mod_ref[0, 0]
    h = _modulate(x_ref[...], g_ref[...], mod[3:4], mod[4:5])
    h_ref[...] = h.astype(BF16)
    logits = _dot3(h, wr_ref[...]) + br_ref[...]
    lt = logits.T[0:N_EXPERTS]
    eio = lax.broadcasted_iota(jnp.int32, lt.shape, 0)
    vals, hots = [], []
    for kk in range(TOP_K):
        m = jnp.max(lt, axis=0, keepdims=True)
        ik = jnp.min(jnp.where(lt == m, eio, N_EXPERTS), axis=0, keepdims=True)
        hot = eio == ik
        lt = jnp.where(hot, -jnp.inf, lt)
        vals.append(m)
        hots.append(hot)
    es = [jnp.exp(v - vals[0]) for v in vals]
    denom = es[0] + es[1] + es[2] + es[3]
    member = jnp.zeros(lt.shape, F32)
    for hot in hots:
        member = member + hot.astype(F32)
    member_bf = member.astype(BF16)

    r_t = lax.broadcasted_iota(jnp.int32, (_TT, _TT), 0)
    c_t = lax.broadcasted_iota(jnp.int32, (_TT, _TT), 1)
    before = (r_t < c_t).astype(BF16)
    in_run = _dot(member_bf, before)
    run_col = _pad_run(jnp.sum(member, axis=1, keepdims=True))
    r_e = lax.broadcasted_iota(jnp.int32, (N_EXPERTS, N_EXPERTS), 0)
    c_e = lax.broadcasted_iota(jnp.int32, (N_EXPERTS, N_EXPERTS), 1)
    first_col = _dot((c_e < r_e).astype(BF16),
                     jnp.broadcast_to(run_col, (N_EXPERTS, _LANES)).astype(BF16))[:, 0:1]
    slot = first_col + in_run
    rsub = lax.broadcasted_iota(jnp.int32, (_LANES, _TT), 0)
    slot_rows = jnp.zeros((_LANES, _TT), F32)
    w_rows = jnp.zeros((_LANES, _TT), F32)
    for kk in range(TOP_K):
        sk = jnp.sum(jnp.where(hots[kk], slot, 0.0), axis=0, keepdims=True)
        slot_rows = jnp.where(rsub == kk, sk, slot_rows)
        w_rows = jnp.where(rsub == kk, es[kk] / denom, w_rows)
    slot_ref[...] = slot_rows[0:8].astype(jnp.int32)
    slotc_ref[...] = slot_rows.T.astype(jnp.int32)
    w_ref[...] = w_rows.T

    member_pad = jnp.concatenate(
        [member_bf, jnp.zeros((_LANES - N_EXPERTS, _TT), BF16)], axis=0)
    run_row = _pad_run(_dot_nt(jnp.ones((8, _TT), BF16), member_pad))
    r_l = lax.broadcasted_iota(jnp.int32, (_LANES, _LANES), 0)
    c_l = lax.broadcasted_iota(jnp.int32, (_LANES, _LANES), 1)
    first_row = _dot(run_row.astype(BF16), (r_l < c_l).astype(BF16))
    earlier = carry_ref[...]
    tsub = lax.broadcasted_iota(jnp.int32, (8, _LANES), 0)
    tab = jnp.where(tsub == 0, run_row * (1.0 / _RUN),
                    jnp.where(tsub == 1, first_row, jnp.where(tsub == 2, earlier, 0.0)))
    tab_ref[0] = tab.astype(jnp.int32)
    carry_ref[...] = earlier + run_row
    tot_ref[...] = earlier + run_row


def _router(xf, mod, g, wr, br, nt, ntb, xrow, seg):
    d = xf.shape[1]
    n = nt * _TT
    return pl.pallas_call(
        _router_kernel,
        grid=(nt,),
        in_specs=[
            pl.BlockSpec((_TT, d), lambda s: (xrow(s), 0)),
            pl.BlockSpec((1, 1, N_MOD, d), lambda s: (s // ntb, seg(s), 0, 0)),
            pl.BlockSpec((1, d), lambda s: (0, 0)),
            pl.BlockSpec((d, _LANES), lambda s: (0, 0)),
            pl.BlockSpec((1, _LANES), lambda s: (0, 0)),
        ],
        out_specs=[
            pl.BlockSpec((_TT, d), lambda s: (s, 0)),
            pl.BlockSpec((8, _TT), lambda s: (0, s)),
            pl.BlockSpec((_TT, _LANES), lambda s: (s, 0)),
            pl.BlockSpec((_TT, _LANES), lambda s: (s, 0)),
            pl.BlockSpec((1, 8, _LANES), lambda s: (s, 0, 0)),
            pl.BlockSpec((8, _LANES), lambda s: (0, 0)),
        ],
        out_shape=[
            jax.ShapeDtypeStruct((n, d), BF16),
            jax.ShapeDtypeStruct((8, n), jnp.int32),
            jax.ShapeDtypeStruct((n, _LANES), jnp.int32),
            jax.ShapeDtypeStruct((n, _LANES), F32),
            jax.ShapeDtypeStruct((nt, 8, _LANES), jnp.int32),
            jax.ShapeDtypeStruct((8, _LANES), F32),
        ],
        scratch_shapes=[pltpu.VMEM((8, _LANES), F32)],
        compiler_params=_cparams("arbitrary"),
        name="moe_router",
    )(xf, mod, g, wr, br)


def _block_sizes(max_groups):
    return tuple(1 << b for b in range(max_groups.bit_length() - 1, -1, -1))


def _for_each_block(groups, sizes, fn):
    for size in sizes:
        shift = size.bit_length()
        done = ((groups >> shift) << shift) * _RUN

        @pl.when((groups & size) != 0)
        def _():
            fn(done, size * _RUN)


def _for_each_run_block(groups, fn):
    big = groups >> 3

    @pl.when(big > 0)
    def _():
        def body(i, carry):
            fn(i * (8 * _RUN), 8 * _RUN)
            return carry

        lax.fori_loop(0, big, body, 0)

    _for_each_block(groups & 7, (4, 2, 1), lambda off, rows: fn(big * (8 * _RUN) + off, rows))


def _dispatch_kernel(tab_ref, tail_ref, slot_ref, h_ref, xs_ref, xt_ref, used_ref, sem):
    s = pl.program_id(0)
    last = pl.num_programs(0) - 1
    buf = s % 2

    def wait_rows(bi, groups):
        _for_each_block(groups, _block_sizes(_SLOTS // _RUN), lambda off, rows: pltpu.make_async_copy(
            xt_ref.at[bi, pl.ds(0, rows)], xs_ref.at[pl.ds(0, rows)], sem.at[bi]).wait())

    @pl.when(s >= 2)
    def _():
        wait_rows(buf, used_ref[buf])

    slots = slot_ref[...]
    sio = lax.broadcasted_iota(jnp.int32, (_SLOTS, _TT), 0)
    perm = jnp.zeros((_SLOTS, _TT), F32)
    for kk in range(TOP_K):
        perm = perm + (sio == slots[kk:kk + 1]).astype(F32)
    xt_ref[buf] = _dot(perm.astype(BF16), h_ref[...].astype(BF16))

    def issue(e, carry):
        first = tab_ref[0, 1, e]
        base = tab_ref[0, 2, e]

        def block(off, rows):
            pltpu.make_async_copy(
                xt_ref.at[buf, pl.ds(pl.multiple_of(first + off, _RUN), rows)],
                xs_ref.at[pl.ds(pl.multiple_of(base + off, _RUN), rows)], sem.at[buf]).start()

        _for_each_run_block(tab_ref[0, 0, e], block)
        return carry

    lax.fori_loop(0, N_EXPERTS, issue, 0)
    used = tab_ref[0, 1, N_EXPERTS - 1] // _RUN + tab_ref[0, 0, N_EXPERTS - 1]
    used_ref[buf] = used

    @pl.when(s == last)
    def _():
        @pl.when(s >= 1)
        def _():
            wait_rows(1 - buf, used_ref[1 - buf])

        wait_rows(buf, used)
        xt_ref[0, 0:_TM, :] = jnp.zeros((_TM, xt_ref.shape[2]), F32)

        def tail_blocks(e, act):
            base = tail_ref[1, e]

            def block(off, rows):
                act(pltpu.make_async_copy(
                    xt_ref.at[0, pl.ds(0, rows)],
                    xs_ref.at[pl.ds(pl.multiple_of(base + off, _RUN), rows)], sem.at[0]))

            _for_each_block(tail_ref[0, e], _block_sizes(_TM // _RUN - 1), block)

        def issue_tail(e, carry):
            tail_blocks(e, lambda cp: cp.start())
            return carry

        def drain_tail(e, carry):
            tail_blocks(e, lambda cp: cp.wait())
            return carry

        lax.fori_loop(0, N_EXPERTS, issue_tail, 0)
        lax.fori_loop(0, N_EXPERTS, drain_tail, 0)

        def free_tile(i):
            row0 = pl.multiple_of(tail_ref[2, 1] + i * _TM, _TM)
            return pltpu.make_async_copy(xt_ref.at[0, pl.ds(0, _TM)], xs_ref.at[pl.ds(row0, _TM)],
                                         sem.at[0])

        def issue_free(i, carry):
            free_tile(i).start()
            return carry

        def drain_free(i, carry):
            free_tile(i).wait()
            return carry

        lax.fori_loop(0, tail_ref[2, 0], issue_free, 0)
        lax.fori_loop(0, tail_ref[2, 0], drain_free, 0)


def _dispatch(tab, tail, slot, h, rows):
    n, d = h.shape
    nt = n // _TT
    return pl.pallas_call(
        _dispatch_kernel,
        grid=(nt,),
        in_specs=[
            pl.BlockSpec((1, 8, _LANES), lambda s: (s, 0, 0), memory_space=pltpu.SMEM),
            pl.BlockSpec((8, _LANES), lambda s: (0, 0), memory_space=pltpu.SMEM),
            pl.BlockSpec((8, _TT), lambda s: (0, s)),
            pl.BlockSpec((_TT, d), lambda s: (s, 0)),
        ],
        out_specs=pl.BlockSpec(memory_space=pl.ANY),
        out_shape=jax.ShapeDtypeStruct((rows, d), F32),
        scratch_shapes=[pltpu.VMEM((2, _SLOTS, d), F32), pltpu.SMEM((2,), jnp.int32),
                        pltpu.SemaphoreType.DMA((2,))],
        compiler_params=_cparams("arbitrary"),
        name="moe_dispatch",
    )(tab, tail, slot, h)


def _expert_kernel(ts_ref, xs_ref, wgu_ref, bgu_ref, wd_ref, bd_ref, ys_ref,
                   wgu_bf_ref, wd_bf_ref, xb_ref, yb_ref, sem_in, sem_out):
    e = pl.program_id(0)
    t0 = ts_ref[e]
    n = ts_ref[e + 1] - t0

    def tile_in(i, slot):
        row0 = pl.multiple_of((t0 + i) * _TM, _TM)
        return pltpu.make_async_copy(xs_ref.at[pl.ds(row0, _TM)], xb_ref.at[slot], sem_in.at[slot])

    def tile_out(i, slot):
        row0 = pl.multiple_of((t0 + i) * _TM, _TM)
        return pltpu.make_async_copy(yb_ref.at[slot], ys_ref.at[pl.ds(row0, _TM)], sem_out.at[slot])

    @pl.when(n > 0)
    def _():
        tile_in(0, 0).start()
        wgu_bf_ref[...] = wgu_ref[0, 0].astype(BF16)
        wd_bf_ref[...] = wd_ref[0, 0].astype(BF16)

        def body(i, carry):
            slot = i % 2
            tile_in(i, slot).wait()

            @pl.when(i + 1 < n)
            def _():
                tile_in(i + 1, 1 - slot).start()

            @pl.when(i >= 2)
            def _():
                tile_out(i - 2, slot).wait()

            x = xb_ref[slot].astype(BF16)
            gu = _dot(x, wgu_bf_ref[...]) + bgu_ref[0, 0]
            gate = jnp.minimum(gu[:, 0:D_EXPERT], SWIGLU_LIMIT)
            up = jnp.clip(gu[:, D_EXPERT:], -SWIGLU_LIMIT, SWIGLU_LIMIT)
            act = (up + 1.0) * (gate * _sigmoid(SWIGLU_ALPHA * gate))
            yb_ref[slot] = _dot(act.astype(BF16), wd_bf_ref[...]) + bd_ref[0, 0]
            tile_out(i, slot).start()
            return carry

        lax.fori_loop(0, n, body, 0)

        @pl.when(n >= 2)
        def _():
            tile_out(n - 2, n % 2).wait()

        tile_out(n - 1, (n - 1) % 2).wait()

    @pl.when(e == pl.num_programs(0) - 1)
    def _():
        yb_ref[0] = jnp.zeros(yb_ref.shape[1:], F32)
        n_all = ys_ref.shape[0] // _TM

        def free_tile(i, carry):
            row0 = pl.multiple_of(i * _TM, _TM)
            cp = pltpu.make_async_copy(yb_ref.at[0], ys_ref.at[pl.ds(row0, _TM)], sem_out.at[0])
            cp.start()
            cp.wait()
            return carry

        lax.fori_loop(ts_ref[e + 1], n_all, free_tile, 0)


def _experts(tile_start, xs, layer, wgu, bgu, wd, bd):
    p, d = xs.shape
    depth, e, _, n2 = wgu.shape
    grid_spec = pltpu.PrefetchScalarGridSpec(
        num_scalar_prefetch=1,
        grid=(e,),
        in_specs=[
            pl.BlockSpec(memory_space=pl.ANY),
            pl.BlockSpec((1, 1, d, n2), lambda i, ts: (layer, i, 0, 0)),
            pl.BlockSpec((1, 1, 1, n2), lambda i, ts: (layer, i, 0, 0)),
            pl.BlockSpec((1, 1, D_EXPERT, d), lambda i, ts: (layer, i, 0, 0)),
            pl.BlockSpec((1, 1, 1, d), lambda i, ts: (layer, i, 0, 0)),
        ],
        out_specs=pl.BlockSpec(memory_space=pl.ANY),
        scratch_shapes=[pltpu.VMEM((d, n2), BF16), pltpu.VMEM((D_EXPERT, d), BF16),
                        pltpu.VMEM((2, _TM, d), F32), pltpu.VMEM((2, _TM, d), F32),
                        pltpu.SemaphoreType.DMA((2,)), pltpu.SemaphoreType.DMA((2,))],
    )
    return pl.pallas_call(
        _expert_kernel,
        grid_spec=grid_spec,
        out_shape=jax.ShapeDtypeStruct((p, d), F32),
        compiler_params=_cparams("arbitrary"),
        name="moe_experts",
    )(tile_start, xs, wgu, bgu.reshape(depth, e, 1, n2), wd, bd.reshape(depth, e, 1, d))


def _combine_kernel(tab_ref, tabn_ref, slotc_ref, w_ref, ys_ref, x_ref, mod_ref, o_ref, yb_ref, sem):
    s = pl.program_id(0)
    buf = s % 2

    def issue_runs(t_ref, bi):
        def issue(e, carry):
            first = t_ref[0, 1, e]
            base = t_ref[0, 2, e]

            def block(off, rows):
                pltpu.make_async_copy(
                    ys_ref.at[pl.ds(pl.multiple_of(base + off, _RUN), rows)],
                    yb_ref.at[bi, pl.ds(pl.multiple_of(first + off, _RUN), rows)], sem.at[bi]).start()

            _for_each_run_block(t_ref[0, 0, e], block)
            return carry

        lax.fori_loop(0, N_EXPERTS, issue, 0)

    @pl.when(s == 0)
    def _():
        yb_ref[...] = jnp.zeros_like(yb_ref)
        issue_runs(tab_ref, 0)

    @pl.when(s + 1 < pl.num_programs(0))
    def _():
        issue_runs(tabn_ref, 1 - buf)

    used = tab_ref[0, 1, N_EXPERTS - 1] + tab_ref[0, 0, N_EXPERTS - 1] * _RUN
    _for_each_block(used // _RUN, _block_sizes(_SLOTS // _RUN), lambda off, rows: pltpu.make_async_copy(
        ys_ref.at[pl.ds(0, rows)], yb_ref.at[buf, pl.ds(0, rows)], sem.at[buf]).wait())
    y = yb_ref[buf].astype(BF16)
    lio = lax.broadcasted_iota(jnp.int32, (_TT, _SLOTS), 1)
    slotc = slotc_ref[...]
    w = w_ref[...]
    gates = jnp.zeros((_TT, _SLOTS), F32)
    for kk in range(TOP_K):
        gates = gates + jnp.where(lio == slotc[:, kk:kk + 1], w[:, kk:kk + 1], 0.0)
    f = _dot(gates.astype(BF16), y)
    o_ref[...] = x_ref[...] + mod_ref[0, 0][5:6] * f


def _combine(tab, slotc, w, ys, xf, mod, nt, ntb, xrow, seg):
    d = xf.shape[1]
    n = nt * _TT
    return pl.pallas_call(
        _combine_kernel,
        grid=(nt,),
        in_specs=[
            pl.BlockSpec((1, 8, _LANES), lambda s: (s, 0, 0), memory_space=pltpu.SMEM),
            pl.BlockSpec((1, 8, _LANES), lambda s: (jnp.minimum(s + 1, nt - 1), 0, 0),
                         memory_space=pltpu.SMEM),
            pl.BlockSpec((_TT, _LANES), lambda s: (s, 0)),
            pl.BlockSpec((_TT, _LANES), lambda s: (s, 0)),
            pl.BlockSpec(memory_space=pl.ANY),
            pl.BlockSpec((_TT, d), lambda s: (xrow(s), 0)),
            pl.BlockSpec((1, 1, N_MOD, d), lambda s: (s // ntb, seg(s), 0, 0)),
        ],
        out_specs=pl.BlockSpec((_TT, d), lambda s: (s, 0)),
        out_shape=jax.ShapeDtypeStruct((n, d), F32),
        scratch_shapes=[pltpu.VMEM((2, _SLOTS, d), F32), pltpu.SemaphoreType.DMA((2,))],
        compiler_params=_cparams("arbitrary"),
        name="moe_combine",
    )(tab, tab, slotc, w, ys, xf, mod)


def _moe(xa, mod, g2, wr, br, layer, wgu, bgu, wd, bd, nct, skip=0):
    b, t, d = xa.shape
    ntb_all = t // _TT
    ntb = ntb_all - skip
    nt = b * ntb
    n = nt * _TT
    xf = xa.reshape(b * t, d)
    xrow = lambda s: (s // ntb) * ntb_all + skip + s % ntb
    seg = lambda s: jnp.where(skip + s % ntb >= nct, 1, 0)
    h, slot, slotc, w, tab, tot = _router(xf, mod, g2, wr, br, nt, ntb, xrow, seg)
    run_rows = tot[0].astype(jnp.int32)
    padded = (run_rows + _TM - 1) // _TM * _TM
    ends = jnp.cumsum(padded)
    starts = ends - padded
    row = lax.broadcasted_iota(jnp.int32, (8, _LANES), 0)
    tab = tab + jnp.where(row == 2, starts[None, :], 0)[None]
    max_rows = n * TOP_K + nt * N_EXPERTS * (_RUN - 1) + N_EXPERTS * (_TM - 1)
    n_tiles = max_rows // _TM
    lane = lax.broadcasted_iota(jnp.int32, (8, _LANES), 1)
    used_rows = ends[N_EXPERTS - 1]
    free = jnp.where(lane == 0, n_tiles - used_rows // _TM, jnp.where(lane == 1, used_rows, 0))
    tail = jnp.where(row == 0, ((padded - run_rows) // _RUN)[None, :],
                     jnp.where(row == 1, (starts + run_rows)[None, :],
                               jnp.where(row == 2, free, 0)))
    tile_start = (jnp.concatenate([starts[:N_EXPERTS], ends[N_EXPERTS - 1:N_EXPERTS]])
                  // _TM).astype(jnp.int32)
    xs = _dispatch(tab, tail, slot, h, n_tiles * _TM)
    ys = _experts(tile_start, xs, layer, wgu, bgu, wd, bd)
    return _combine(tab, slotc, w, ys, xf, mod, nt, ntb, xrow, seg).reshape(b, ntb * _TT, d)


def _rope_tables(n_ctx, n_lat):
    rows = n_lat // GRID_W
    row_ids = jnp.repeat(jnp.arange(rows), GRID_W).astype(F32)
    col_ids = jnp.tile(jnp.arange(GRID_W), rows).astype(F32)
    axis_dim = HEAD_DIM // 2
    inv_freq = 1.0 / (ROPE_THETA ** (jnp.arange(0, axis_dim, 2, dtype=F32) / axis_dim))
    ang = jnp.concatenate([row_ids[:, None] * inv_freq, col_ids[:, None] * inv_freq], axis=-1)
    cos = jnp.concatenate([jnp.ones((n_ctx, axis_dim), F32), jnp.cos(ang)], axis=0)
    sin = jnp.concatenate([jnp.zeros((n_ctx, axis_dim), F32), jnp.sin(ang)], axis=0)
    return cos.T, sin.T


def kernel(x, c, ctx, c_ctx, ada_w, ada_b, norm1_g, norm2_g, attn_w_qkv, attn_q_gain, attn_k_gain, attn_w_o, gla_w_in, gla_w_gk2_f, gla_b_gk_f, gla_w_gk2_b, gla_b_gk_b, gla_o_gain, gla_w_o, moe_w_router, moe_b_router, moe_w_gu, moe_b_gu, moe_w_down, moe_b_down):
    b, n_lat, d = x.shape
    n_ctx = ctx.shape[1]
    assert d == D_MODEL and n_ctx % _TT == 0 and n_lat % _TT == 0 and n_lat % GRID_W == 0
    nct = n_ctx // _TT
    depth = ada_w.shape[0]

    r = -(-(b + 1) // 8) * 8
    cond = jnp.zeros((r, d), F32).at[:b].set(c).at[b].set(c_ctx)
    table = _ada_table(cond, ada_w, ada_b).reshape(depth, r, N_MOD, d)
    mods = jnp.stack([jnp.broadcast_to(table[:, b:b + 1], (depth, b, N_MOD, d)), table[:, :b]],
                     axis=2)

    cos_t, sin_t = _rope_tables(n_ctx, n_lat)
    perm = jnp.concatenate([jnp.arange(0, HEAD_DIM, 2), jnp.arange(1, HEAD_DIM, 2)])
    qk_cols = (jnp.arange(N_Q_HEADS + N_KV_HEADS)[:, None] * HEAD_DIM + perm[None, :]).reshape(-1)
    cols = jnp.concatenate([qk_cols, jnp.arange(Q_DIM + KV_DIM, QKV_DIM)])

    xa = jnp.concatenate([ctx, x], axis=1)
    for i in range(depth):
        mod = mods[i]
        j = i // N_MIXERS
        g1 = norm1_g[i].reshape(1, d)
        if i % N_MIXERS == 0:
            wt = attn_w_qkv[j][:, cols].T.astype(BF16)
            gq = attn_q_gain[j][perm].reshape(HEAD_DIM, 1)
            gk = attn_k_gain[j][perm].reshape(HEAD_DIM, 1)
            qt, k, vt = _qkv_project(xa, mod, g1, wt, gq, gk, cos_t, sin_t, nct)
            xa = _attention(qt, k, vt, attn_w_o[j].astype(BF16), xa, mod, nct)
        else:
            n_main = 2 * GLA_DK + 2 * GLA_DV
            wm = gla_w_in[j][:, :n_main].astype(BF16)
            wl = jnp.zeros((d, _LANES), F32).at[:, :2 * GLA_GATE_RANK].set(
                gla_w_in[j][:, n_main:]).astype(BF16)
            w2 = jnp.zeros((_LANES, 2 * GLA_DK), F32)
            w2 = w2.at[:GLA_GATE_RANK, :GLA_DK].set(gla_w_gk2_f[j])
            w2 = w2.at[GLA_GATE_RANK:2 * GLA_GATE_RANK, GLA_DK:].set(gla_w_gk2_b[j])
            b2 = jnp.concatenate([gla_b_gk_f[j], gla_b_gk_b[j]]).reshape(1, 2 * GLA_DK)
            q, k, v, rr, gf, gb = _gla_project(xa, mod, g1, wm, wl, w2, b2, nct)
            o_f, o_b = _gla_scan(q, k, v, gf, gb, nct)
            xa = _gla_readout(o_f, o_b, rr, gla_o_gain[j].reshape(1, GLA_DV_HEAD),
                              gla_w_o[j].astype(BF16), xa, mod, nct)
        wr = jnp.zeros((d, _LANES), F32).at[:, :N_EXPERTS].set(moe_w_router[i])
        br = jnp.zeros((1, _LANES), F32).at[0, :N_EXPERTS].set(moe_b_router[i])
        if i == depth - 1:
            return _moe(xa, mod, norm2_g[i].reshape(1, d), wr, br,
                        i, moe_w_gu, moe_b_gu, moe_w_down, moe_b_down, nct, skip=nct)
        xa = _moe(xa, mod, norm2_g[i].reshape(1, d), wr, br,
                  i, moe_w_gu, moe_b_gu, moe_w_down, moe_b_down, nct)
    return xa[:, n_ctx:]
```
